```python
import math
import jax, jax.numpy as jnp
from jax import lax
import numpy as np

D_MODEL = 1024
BATCH = 8
SEQ = 4096
DEPTH = 2

N_META = 16
Q_BLOCK = 128
NORM_EPS = 1e-6
D_FF = 2816

LRU_WIDTH = 256
LRU_BLOCKS = 4
LRU_BLOCK_DIM = LRU_WIDTH // LRU_BLOCKS
CONV_WIDTH = 4
LRU_C = 8.0
LRU_A_MIN = 0.9
LRU_A_MAX = 0.999

DIFF_HEADS = 4
DIFF_QK_DIM = 64
DIFF_V_DIM = 2 * DIFF_QK_DIM

MLA_HEADS = 4
MLA_Q_LORA = 256
MLA_KV_LORA = 128
MLA_NOPE_DIM = 64
MLA_ROPE_DIM = 32
MLA_V_DIM = 64
ROPE_THETA = 10000.0

IN_WIDTHS = (LRU_WIDTH, LRU_WIDTH,
             2 * DIFF_HEADS * DIFF_QK_DIM, 2 * DIFF_HEADS * DIFF_QK_DIM, DIFF_HEADS * DIFF_V_DIM,
             MLA_Q_LORA, MLA_KV_LORA, MLA_ROPE_DIM)
D_IN = 2 * LRU_WIDTH + 4 * DIFF_HEADS * DIFF_QK_DIM + DIFF_HEADS * DIFF_V_DIM + MLA_Q_LORA + MLA_KV_LORA + MLA_ROPE_DIM
D_MIX = LRU_WIDTH + DIFF_HEADS * DIFF_V_DIM + MLA_HEADS * MLA_V_DIM

kernel_name = "hymba_style_rglru_diffattn_mla_macaron"


def rmsnorm(x, g):
    x32 = x.astype(jnp.float32)
    y = x32 * lax.rsqrt(jnp.mean(x32 * x32, axis=-1, keepdims=True) + NORM_EPS)
    return (y * g.astype(jnp.float32)).astype(x.dtype)


def swiglu_ffn(x, w_in, w_out):
    gate, up = jnp.split(x @ w_in, 2, axis=-1)
    return (jax.nn.silu(gate) * up) @ w_out


def apply_rope(x, pos):
    half = x.shape[-1] // 2
    inv_freq = ROPE_THETA ** (-jnp.arange(half, dtype=jnp.float32) / half)
    ang = pos.astype(jnp.float32)[:, None] * inv_freq[None, :]
    cos = jnp.cos(ang)[None, :, None, :]
    sin = jnp.sin(ang)[None, :, None, :]
    x32 = x.astype(jnp.float32)
    x1, x2 = x32[..., :half], x32[..., half:]
    return jnp.concatenate([x1 * cos - x2 * sin, x2 * cos + x1 * sin], axis=-1).astype(x.dtype)


def causal_attention(q, k, v, scale):
    B, T = q.shape[0], q.shape[1]
    n_blocks = T // Q_BLOCK
    q_blocks = jnp.moveaxis(q.reshape(B, n_blocks, Q_BLOCK, *q.shape[2:]), 1, 0)
    k_pos = jnp.arange(T)

    def block(args):
        q_blk, blk_idx = args
        s = jnp.einsum('bqmhd,bkmhd->bmhqk', q_blk, k, preferred_element_type=jnp.float32) * scale
        q_pos = blk_idx * Q_BLOCK + jnp.arange(Q_BLOCK)
        s = jnp.where(k_pos[None, :] <= q_pos[:, None], s, -jnp.inf)
        p = jax.nn.softmax(s, axis=-1).astype(v.dtype)
        return jnp.einsum('bmhqk,bkhd->bqmhd', p, v)

    out = lax.map(block, (q_blocks, jnp.arange(n_blocks)))
    return jnp.moveaxis(out, 0, 1).reshape(B, T, *out.shape[3:])


def causal_depthwise_conv(u, w, b):
    out = lax.conv_general_dilated(
        u, w[:, None, :].astype(u.dtype), window_strides=(1,),
        padding=[(CONV_WIDTH - 1, 0)], dimension_numbers=('NWC', 'WIO', 'NWC'),
        feature_group_count=u.shape[-1])
    return out + b.astype(u.dtype)


def rg_lru(u, wa, ba, wx, bx, lam):
    B, T, C = u.shape
    ub = u.reshape(B, T, LRU_BLOCKS, LRU_BLOCK_DIM)
    r = jax.nn.sigmoid(jnp.einsum('btnc,ncd->btnd', ub, wa) + ba).reshape(B, T, C)
    i = jax.nn.sigmoid(jnp.einsum('btnc,ncd->btnd', ub, wx) + bx).reshape(B, T, C)
    log_a = -LRU_C * r.astype(jnp.float32) * jax.nn.softplus(-lam.astype(jnp.float32))
    a = jnp.exp(log_a)
    b = jnp.sqrt(-jnp.expm1(2.0 * log_a)) * (i * u).astype(jnp.float32)

    def combine(left, right):
        a1, b1 = left
        a2, b2 = right
        return a1 * a2, a2 * b1 + b2

    _, h = lax.associative_scan(combine, (a, b), axis=1)
    return h.astype(u.dtype)


def token_mixer(h, layer_idx, pos, w_in, conv_w, conv_b, lru_wa, lru_ba, lru_wx, lru_bx,
                lru_lambda, lru_out_norm, lam_q1, lam_k1, lam_q2, lam_k2, diff_subln,
                q_norm, w_uq, kv_norm, w_ukv, mla_out_norm, w_out):
    B, T, _ = h.shape
    offsets = [int(o) for o in np.cumsum(IN_WIDTHS)[:-1]]
    g_a, u_a, q_d, k_d, v_d, c_q, c_kv, k_r = jnp.split(h @ w_in, offsets, axis=-1)

    u_a = causal_depthwise_conv(u_a, conv_w, conv_b)
    y_a = rg_lru(u_a, lru_wa, lru_ba, lru_wx, lru_bx, lru_lambda) * jax.nn.gelu(g_a)
    y_a = rmsnorm(y_a, lru_out_norm)

    q_d = q_d.reshape(B, T, DIFF_HEADS, 2, DIFF_QK_DIM).swapaxes(2, 3)
    k_d = k_d.reshape(B, T, DIFF_HEADS, 2, DIFF_QK_DIM).swapaxes(2, 3)
    v_d = v_d.reshape(B, T, DIFF_HEADS, DIFF_V_DIM)
    o_d = causal_attention(q_d, k_d, v_d, DIFF_QK_DIM ** -0.5)
    lam_init = 0.8 - 0.6 * math.exp(-0.3 * layer_idx)
    lam = (jnp.exp(jnp.sum(lam_q1.astype(jnp.float32) * lam_k1.astype(jnp.float32)))
           - jnp.exp(jnp.sum(lam_q2.astype(jnp.float32) * lam_k2.astype(jnp.float32)))
           + lam_init).astype(h.dtype)
    o_d = o_d[:, :, 0] - lam * o_d[:, :, 1]
    y_b = (rmsnorm(o_d, diff_subln) * (1.0 - lam_init)).reshape(B, T, DIFF_HEADS * DIFF_V_DIM)

    q = (rmsnorm(c_q, q_norm) @ w_uq).reshape(B, T, MLA_HEADS, MLA_NOPE_DIM + MLA_ROPE_DIM)
    q = jnp.concatenate([q[..., :MLA_NOPE_DIM], apply_rope(q[..., MLA_NOPE_DIM:], pos)], axis=-1)
    kv = (rmsnorm(c_kv, kv_norm) @ w_ukv).reshape(B, T, MLA_HEADS, MLA_NOPE_DIM + MLA_V_DIM)
    k_nope, v_c = kv[..., :MLA_NOPE_DIM], kv[..., MLA_NOPE_DIM:]
    k_rope = jnp.broadcast_to(apply_rope(k_r[:, :, None, :], pos), (B, T, MLA_HEADS, MLA_ROPE_DIM))
    k = jnp.concatenate([k_nope, k_rope], axis=-1)
    o_c = causal_attention(q[:, :, None], k[:, :, None], v_c,
                           (MLA_NOPE_DIM + MLA_ROPE_DIM) ** -0.5)[:, :, 0]
    y_c = rmsnorm(o_c.reshape(B, T, MLA_HEADS * MLA_V_DIM), mla_out_norm)

    return jnp.concatenate([y_a, y_b, y_c], axis=-1) @ w_out


def setup_inputs(seed: int = 0) -> dict:
    key = jax.random.key(seed)
    ks = iter(jax.random.split(key, 40))
    f32 = jnp.float32

    def nrm(shape, scale):
        return jax.random.normal(next(ks), shape, f32) * scale

    def gain(shape):
        return 1.0 + 0.01 * jax.random.normal(next(ks), shape, f32)

    L = DEPTH
    u = jax.random.uniform(next(ks), (L, LRU_WIDTH), f32, minval=LRU_A_MIN, maxval=LRU_A_MAX)
    s = u ** (1.0 / LRU_C)
    lru_lambda = jnp.log(s) - jnp.log1p(-s)
    return {
        "x": nrm((BATCH, SEQ, D_MODEL), 1.0),
        "meta_tokens": nrm((N_META, D_MODEL), 1.0),
        "ffn1_norm": gain((L, D_MODEL)),
        "ffn1_in": nrm((L, D_MODEL, 2 * D_FF), D_MODEL ** -0.5),
        "ffn1_out": nrm((L, D_FF, D_MODEL), D_FF ** -0.5),
        "mix_norm": gain((L, D_MODEL)),
        "w_in": nrm((L, D_MODEL, D_IN), D_MODEL ** -0.5),
        "conv_w": nrm((L, CONV_WIDTH, LRU_WIDTH), CONV_WIDTH ** -0.5),
        "conv_b": nrm((L, LRU_WIDTH), 0.01),
        "lru_wa": nrm((L, LRU_BLOCKS, LRU_BLOCK_DIM, LRU_BLOCK_DIM), LRU_BLOCK_DIM ** -0.5),
        "lru_ba": nrm((L, LRU_BLOCKS, LRU_BLOCK_DIM), 0.01),
        "lru_wx": nrm((L, LRU_BLOCKS, LRU_BLOCK_DIM, LRU_BLOCK_DIM), LRU_BLOCK_DIM ** -0.5),
        "lru_bx": nrm((L, LRU_BLOCKS, LRU_BLOCK_DIM), 0.01),
        "lru_lambda": lru_lambda,
        "lru_out_norm": gain((L, LRU_WIDTH)),
        "lam_q1": nrm((L, DIFF_QK_DIM), 0.1),
        "lam_k1": nrm((L, DIFF_QK_DIM), 0.1),
        "lam_q2": nrm((L, DIFF_QK_DIM), 0.1),
        "lam_k2": nrm((L, DIFF_QK_DIM), 0.1),
        "diff_subln": gain((L, DIFF_V_DIM)),
        "q_norm": gain((L, MLA_Q_LORA)),
        "w_uq": nrm((L, MLA_Q_LORA, MLA_HEADS * (MLA_NOPE_DIM + MLA_ROPE_DIM)), MLA_Q_LORA ** -0.5),
        "kv_norm": gain((L, MLA_KV_LORA)),
        "w_ukv": nrm((L, MLA_KV_LORA, MLA_HEADS * (MLA_NOPE_DIM + MLA_V_DIM)), MLA_KV_LORA ** -0.5),
        "mla_out_norm": gain((L, MLA_HEADS * MLA_V_DIM)),
        "w_out": nrm((L, D_MIX, D_MODEL), D_MIX ** -0.5),
        "ffn2_norm": gain((L, D_MODEL)),
        "ffn2_in": nrm((L, D_MODEL, 2 * D_FF), D_MODEL ** -0.5),
        "ffn2_out": nrm((L, D_FF, D_MODEL), D_FF ** -0.5),
        "final_norm": gain((D_MODEL,)),
    }


def reference(x, meta_tokens, ffn1_norm, ffn1_in, ffn1_out, mix_norm, w_in, conv_w, conv_b,
              lru_wa, lru_ba, lru_wx, lru_bx, lru_lambda, lru_out_norm, lam_q1, lam_k1,
              lam_q2, lam_k2, diff_subln, q_norm, w_uq, kv_norm, w_ukv, mla_out_norm, w_out,
              ffn2_norm, ffn2_in, ffn2_out, final_norm):
    B, S, D = x.shape
    T = N_META + S
    T_pad = -(-T // Q_BLOCK) * Q_BLOCK
    meta = jnp.broadcast_to(meta_tokens[None].astype(x.dtype), (B, N_META, D))
    h = jnp.concatenate([meta, x], axis=1)
    h = jnp.pad(h, ((0, 0), (0, T_pad - T), (0, 0)))
    pos = jnp.arange(T_pad, dtype=jnp.int32)
    for l in range(DEPTH):
        h = h + 0.5 * swiglu_ffn(rmsnorm(h, ffn1_norm[l]), ffn1_in[l], ffn1_out[l])
        h = h + token_mixer(rmsnorm(h, mix_norm[l]), l, pos, w_in[l], conv_w[l], conv_b[l],
                            lru_wa[l], lru_ba[l], lru_wx[l], lru_bx[l], lru_lambda[l],
                            lru_out_norm[l], lam_q1[l], lam_k1[l], lam_q2[l], lam_k2[l],
                            diff_subln[l], q_norm[l], w_uq[l], kv_norm[l], w_ukv[l],
                            mla_out_norm[l], w_out[l])
        h = h + 0.5 * swiglu_ffn(rmsnorm(h, ffn2_norm[l]), ffn2_in[l], ffn2_out[l])
    h = rmsnorm(h, final_norm)
    return h[:, N_META:N_META + S]
```

```python
import functools
import math

import jax
import jax.numpy as jnp
import numpy as np
from jax import lax
from jax.experimental import pallas as pl
from jax.experimental.pallas import tpu as pltpu

F32 = jnp.float32
BF16 = jnp.bfloat16

N_META = 16
Q_BLOCK = 128
NORM_EPS = 1e-6
CONV_WIDTH = 4
LRU_BLOCKS = 4
LRU_C = 8.0
DIFF_HEADS = 4
DIFF_QK_DIM = 64
MLA_HEADS = 4
MLA_NOPE_DIM = 64
MLA_ROPE_DIM = 32
MLA_V_DIM = 64
ROPE_THETA = 10000.0

LANES = 128
SUBLANES = 8
VMEM_LIMIT_BYTES = 56 * 1024 * 1024


def _params(*semantics):
    return pltpu.CompilerParams(dimension_semantics=semantics, vmem_limit_bytes=VMEM_LIMIT_BYTES)


def _resident(shape):
    zeros = (0,) * len(shape)
    return pl.BlockSpec(shape, lambda *_: zeros, pipeline_mode=pl.Buffered(1))


def _rmsnorm(x, gain):
    return x * lax.rsqrt(jnp.mean(x * x, axis=-1, keepdims=True) + NORM_EPS) * gain


def _ffn_kernel(h_ref, gain_ref, w_in_ref, w_out_ref, *rest, d_ff, ff_chunk, final_norm):
    if final_norm:
        final_gain_ref, o_ref, acc_ref = rest
    else:
        o_ref, acc_ref = rest
    x = h_ref[...]
    xn = _rmsnorm(x, gain_ref[...]).astype(BF16)
    for j in range(d_ff // ff_chunk):
        lo = j * ff_chunk
        gate = jnp.dot(xn, w_in_ref[:, lo:lo + ff_chunk], preferred_element_type=F32)
        up = jnp.dot(xn, w_in_ref[:, d_ff + lo:d_ff + lo + ff_chunk], preferred_element_type=F32)
        act = (jax.nn.silu(gate) * up).astype(BF16)
        part = jnp.dot(act, w_out_ref[lo:lo + ff_chunk, :], preferred_element_type=F32)
        if j == 0:
            acc_ref[...] = part
        else:
            acc_ref[...] += part
    y = x + 0.5 * acc_ref[...]
    if final_norm:
        y = _rmsnorm(y, final_gain_ref[...])
    o_ref[...] = y


def _ffn(h, gain, w_in, w_out, final_gain=None, *, row_tile=512, ff_chunk=256):
    n, d = h.shape
    d_ff = w_out.shape[0]
    assert n % row_tile == 0 and d_ff % ff_chunk == 0
    final_norm = final_gain is not None
    in_specs = [pl.BlockSpec((row_tile, d), lambda i: (i, 0)), _resident((1, d)),
                _resident(w_in.shape), _resident(w_out.shape)]
    args = [h, gain.reshape(1, d), w_in, w_out]
    if final_norm:
        in_specs.append(_resident((1, d)))
        args.append(final_gain.reshape(1, d))
    return pl.pallas_call(
        functools.partial(_ffn_kernel, d_ff=d_ff, ff_chunk=ff_chunk, final_norm=final_norm),
        grid=(n // row_tile,),
        in_specs=in_specs,
        out_specs=pl.BlockSpec((row_tile, d), lambda i: (i, 0)),
        out_shape=jax.ShapeDtypeStruct((n, d), F32),
        scratch_shapes=[pltpu.VMEM((row_tile, d), F32)],
        compiler_params=_params("parallel"),
        name="ffn",
    )(*args)


def _rope(x, cos, sin_lo, sin_hi):
    half = MLA_ROPE_DIM // 2
    return x * cos + pltpu.roll(x, LANES - half, 1) * sin_lo + pltpu.roll(x, half, 1) * sin_hi


def _proj_kernel(h_ref, gain_ref, w_ref, qn_ref, wuq_ref, kvn_ref, wuk_ref, wuv_ref, cos_ref, slo_ref, shi_ref,
                 g_ref, u_ref, qd_ref, kd_ref, vd_ref, qc_ref, kc_ref, vc_ref, *, offs):
    xn = _rmsnorm(h_ref[...], gain_ref[...]).astype(BF16)

    def proj(name):
        lo, hi = offs[name]
        return jnp.dot(xn, w_ref[:, lo:hi], preferred_element_type=F32)

    g_ref[...] = proj("g_a")
    u_ref[...] = proj("u_a")
    qd_ref[...] = (proj("q_d") * (DIFF_QK_DIM ** -0.5)).astype(BF16)
    kd_ref[...] = proj("k_d").astype(BF16)
    vd_ref[...] = proj("v_d").astype(BF16)

    cos, slo, shi = cos_ref[...], slo_ref[...], shi_ref[...]
    cq = _rmsnorm(proj("c_q"), qn_ref[...]).astype(BF16)
    q = jnp.dot(cq, wuq_ref[...], preferred_element_type=F32)
    ckv = _rmsnorm(proj("c_kv"), kvn_ref[...]).astype(BF16)
    k_nope = jnp.dot(ckv, wuk_ref[...], preferred_element_type=F32)
    vc_ref[...] = jnp.dot(ckv, wuv_ref[...], preferred_element_type=F32).astype(BF16)
    k_rope = _rope(proj("k_r"), cos, slo, shi)
    q_scale = (MLA_NOPE_DIM + MLA_ROPE_DIM) ** -0.5
    for hd in range(MLA_HEADS):
        sl = slice(hd * LANES, (hd + 1) * LANES)
        qc_ref[:, sl] = (_rope(q[:, sl], cos, slo, shi) * q_scale).astype(BF16)
        kc_ref[:, sl] = (k_nope[:, sl] + k_rope).astype(BF16)


def _rope_tables(t_len):
    half = MLA_ROPE_DIM // 2
    inv_freq = ROPE_THETA ** (-jnp.arange(half, dtype=F32) / half)
    ang = jnp.arange(t_len, dtype=jnp.int32).astype(F32)[:, None] * inv_freq[None, :]
    cos, sin = jnp.cos(ang), jnp.sin(ang)
    zeros = jnp.zeros((t_len, half), F32)
    ones = jnp.ones((t_len, MLA_NOPE_DIM), F32)
    tail = jnp.zeros((t_len, LANES - MLA_NOPE_DIM - MLA_ROPE_DIM), F32)
    cos_t = jnp.concatenate([ones, cos, cos, tail], axis=1)
    sin_lo = jnp.concatenate([0 * ones, -sin, zeros, tail], axis=1)
    sin_hi = jnp.concatenate([0 * ones, zeros, sin, tail], axis=1)
    return cos_t, sin_lo, sin_hi


def _pad_heads(w, heads, width, lo, hi):
    k = w.shape[0]
    w = w.reshape(k, heads, width)[:, :, lo:hi]
    w = jnp.pad(w, ((0, 0), (0, 0), (0, LANES - (hi - lo))))
    return w.reshape(k, heads * LANES)


def _mixer_proj(h3, gain, w_in, q_norm, w_uq, kv_norm, w_ukv, tables, *, row_tile):
    b, t, d = h3.shape
    assert t % row_tile == 0
    lru_w = w_uq.shape[0]
    del lru_w
    widths = dict(g_a=256, u_a=256, q_d=512, k_d=512, v_d=512, c_q=w_uq.shape[0], c_kv=w_ukv.shape[0],
                  k_r=MLA_ROPE_DIM)
    offs, lo = {}, 0
    for name, wd in widths.items():
        offs[name] = (lo, lo + wd)
        lo += wd
    assert lo == w_in.shape[1]
    kr_lo, kr_hi = offs["k_r"]
    w_kr = jnp.pad(w_in[:, kr_lo:kr_hi], ((0, 0), (MLA_NOPE_DIM, LANES - MLA_NOPE_DIM - MLA_ROPE_DIM)))
    w_all = jnp.concatenate([w_in[:, :kr_lo], w_kr], axis=1).astype(BF16)
    offs["k_r"] = (kr_lo, kr_lo + LANES)
    qk = MLA_NOPE_DIM + MLA_ROPE_DIM
    w_uq_p = _pad_heads(w_uq, MLA_HEADS, qk, 0, qk).astype(BF16)
    w_uk_p = _pad_heads(w_ukv, MLA_HEADS, MLA_NOPE_DIM + MLA_V_DIM, 0, MLA_NOPE_DIM).astype(BF16)
    w_uv = w_ukv.reshape(-1, MLA_HEADS, MLA_NOPE_DIM + MLA_V_DIM)[:, :, MLA_NOPE_DIM:]
    w_uv = w_uv.reshape(-1, MLA_HEADS * MLA_V_DIM).astype(BF16)

    def rows(width, dtype):
        return (pl.BlockSpec((None, row_tile, width), lambda i, j: (i, j, 0)),
                jax.ShapeDtypeStruct((b, t, width), dtype))

    outs = [rows(256, F32), rows(256, F32), rows(512, BF16), rows(512, BF16), rows(512, BF16),
            rows(MLA_HEADS * LANES, BF16), rows(MLA_HEADS * LANES, BF16), rows(MLA_HEADS * MLA_V_DIM, BF16)]
    table_spec = pl.BlockSpec((row_tile, LANES), lambda i, j: (j, 0))
    return pl.pallas_call(
        functools.partial(_proj_kernel, offs=offs),
        grid=(b, t // row_tile),
        in_specs=[pl.BlockSpec((None, row_tile, d), lambda i, j: (i, j, 0)), _resident((1, d)),
                  _resident(w_all.shape), _resident((1, w_uq.shape[0])), _resident(w_uq_p.shape),
                  _resident((1, w_ukv.shape[0])), _resident(w_uk_p.shape), _resident(w_uv.shape),
                  table_spec, table_spec, table_spec],
        out_specs=[o[0] for o in outs],
        out_shape=[o[1] for o in outs],
        compiler_params=_params("parallel", "parallel"),
        name="mixer_proj",
    )(h3, gain.reshape(1, d), w_all, q_norm.reshape(1, -1), w_uq_p, kv_norm.reshape(1, -1), w_uk_p, w_uv, *tables)


def _lru_kernel(g_ref, u_ref, cw_ref, cb_ref, wa_ref, ba_ref, wx_ref, bx_ref, lam_ref, on_ref,
                y_ref, halo_ref, carry_ref, a_scr, b_scr, h_scr, *, chunk):
    @pl.when(pl.program_id(1) == 0)
    def _():
        halo_ref[...] = jnp.zeros_like(halo_ref)
        carry_ref[...] = jnp.zeros_like(carry_ref)

    u = u_ref[...]
    ext = jnp.concatenate([halo_ref[...], u], axis=0)
    halo_ref[...] = u[chunk - SUBLANES:, :]
    conv = cb_ref[...] + cw_ref[CONV_WIDTH - 1:CONV_WIDTH, :] * u
    for j in range(CONV_WIDTH - 1):
        shifted = pltpu.roll(ext, CONV_WIDTH - 1 - j, 0)[SUBLANES:, :]
        conv = conv + cw_ref[j:j + 1, :] * shifted

    ub = conv.astype(BF16)
    r = jax.nn.sigmoid(jnp.dot(ub, wa_ref[...], preferred_element_type=F32) + ba_ref[...])
    i = jax.nn.sigmoid(jnp.dot(ub, wx_ref[...], preferred_element_type=F32) + bx_ref[...])
    neg_lam = -lam_ref[...]
    softplus = jnp.maximum(neg_lam, 0.0) + jnp.log1p(jnp.exp(-jnp.abs(neg_lam)))
    log_a = -LRU_C * r * softplus
    a = jnp.exp(log_a)
    b = jnp.sqrt(-jnp.tanh(log_a) * (a * a + 1.0)) * (i * conv)

    row = jnp.bitwise_and(lax.broadcasted_iota(jnp.int32, a.shape, 0), SUBLANES - 1)
    for s in (1, 2, 4):
        keep = row >= s
        b = jnp.where(keep, a * pltpu.roll(b, s, 0) + b, b)
        a = jnp.where(keep, a * pltpu.roll(a, s, 0), a)
    a_scr[...] = a
    b_scr[...] = b

    def group(gi, carry):
        sl = pl.ds(pl.multiple_of(gi * SUBLANES, SUBLANES), SUBLANES)
        h = a_scr[sl, :] * carry + b_scr[sl, :]
        h_scr[sl, :] = h
        return jnp.broadcast_to(h[SUBLANES - 1:SUBLANES, :], h.shape)

    carry_ref[...] = lax.fori_loop(0, chunk // SUBLANES, group, carry_ref[...])
    y = h_scr[...] * jax.nn.gelu(g_ref[...])
    y_ref[...] = _rmsnorm(y, on_ref[...]).astype(BF16)


def _block_diag(w):
    n, c, d = w.shape
    eye = jnp.eye(n, dtype=w.dtype)
    return (w[:, :, None, :] * eye[:, None, :, None]).reshape(n * c, n * d)


def _lru(g, u, conv_w, conv_b, wa, ba, wx, bx, lam, out_norm, *, chunk):
    b, t, c = u.shape
    assert t % chunk == 0 and chunk % (2 * SUBLANES) == 0
    row = lambda v: v.reshape(1, c)
    seq = pl.BlockSpec((None, chunk, c), lambda i, j: (i, j, 0))
    return pl.pallas_call(
        functools.partial(_lru_kernel, chunk=chunk),
        grid=(b, t // chunk),
        in_specs=[seq, seq, _resident((CONV_WIDTH, c)), _resident((1, c)), _resident((c, c)), _resident((1, c)),
                  _resident((c, c)), _resident((1, c)), _resident((1, c)), _resident((1, c))],
        out_specs=seq,
        out_shape=jax.ShapeDtypeStruct((b, t, c), BF16),
        scratch_shapes=[pltpu.VMEM((SUBLANES, c), F32), pltpu.VMEM((SUBLANES, c), F32),
                        pltpu.VMEM((chunk, c), F32), pltpu.VMEM((chunk, c), F32), pltpu.VMEM((chunk, c), F32)],
        compiler_params=_params("parallel", "arbitrary"),
        name="rg_lru",
    )(g, u, conv_w, row(conv_b), _block_diag(wa).astype(BF16), row(ba), _block_diag(wx).astype(BF16), row(bx),
      row(lam), row(out_norm))


def _attn_kernel(q_ref, k_ref, v_ref, *rest, blk, mode, lam_init):
    if mode == "diff":
        lam_ref, subln_ref, o_ref, m_scr, l_scr, acc_scr = rest
    else:
        o_ref, m_scr, l_scr, acc_scr = rest
    qi = pl.program_id(2)
    lane = lax.broadcasted_iota(jnp.int32, (blk, LANES), 1)
    low = lane < (LANES // 2)
    if mode == "diff":
        q = q_ref[...]
        zero = jnp.zeros_like(q)
        qs = (jnp.where(low, q, zero), jnp.where(low, zero, q))
    else:
        qs = (q_ref[:, :LANES], q_ref[:, LANES:])

    m_scr[...] = jnp.full(m_scr.shape, -jnp.inf, F32)
    l_scr[...] = jnp.zeros_like(l_scr)
    acc_scr[...] = jnp.zeros_like(acc_scr)

    def step(kj, masked):
        rows = pl.ds(pl.multiple_of(kj * blk, blk), blk)
        v = v_ref[rows, :]
        for mi in range(2):
            k = k_ref[rows, :] if mode == "diff" else k_ref[rows, mi * LANES:(mi + 1) * LANES]
            s = lax.dot_general(qs[mi], k, (((1,), (1,)), ((), ())), preferred_element_type=F32)
            if masked:
                r_id = lax.broadcasted_iota(jnp.int32, s.shape, 0)
                c_id = lax.broadcasted_iota(jnp.int32, s.shape, 1)
                s = jnp.where(c_id <= r_id, s, -jnp.inf)
            m_old = m_scr[mi]
            m_new = jnp.maximum(m_old, jnp.max(s, axis=-1, keepdims=True))
            alpha = jnp.exp(m_old - m_new)
            p = jnp.exp(s - m_new)
            l_scr[mi] = alpha * l_scr[mi] + jnp.sum(p, axis=-1, keepdims=True)
            acc_scr[mi] = alpha * acc_scr[mi] + jnp.dot(p.astype(BF16), v, preferred_element_type=F32)
            m_scr[mi] = m_new

    def body(kj, c):
        step(kj, False)
        return c

    lax.fori_loop(0, qi, body, 0)
    step(qi, True)

    o_a = acc_scr[0] / l_scr[0]
    o_b = acc_scr[1] / l_scr[1]
    if mode == "diff":
        p_ = lam_ref[...]
        lam = (jnp.exp(jnp.sum(p_[0:1] * p_[1:2], axis=-1, keepdims=True))
               - jnp.exp(jnp.sum(p_[2:3] * p_[3:4], axis=-1, keepdims=True)) + lam_init)
        o = _rmsnorm(o_a - lam * o_b, subln_ref[...]) * (1.0 - lam_init)
        o_ref[...] = o.astype(o_ref.dtype)
    else:
        o_ref[...] = jnp.where(low, o_a, o_b).astype(o_ref.dtype)


def _attention(q, k, v, *, mode, blk, lam_params=None, subln=None, lam_init=0.0, out_dtype=BF16):
    b, t, _ = q.shape
    groups = v.shape[2] // LANES
    width = LANES if mode == "diff" else 2 * LANES
    assert t % blk == 0 and q.shape[2] == groups * width
    in_specs = [pl.BlockSpec((None, blk, width), lambda i, g, j: (i, j, g)),
                pl.BlockSpec((None, t, width), lambda i, g, j: (i, 0, g)),
                pl.BlockSpec((None, t, LANES), lambda i, g, j: (i, 0, g))]
    args = [q, k, v]
    if mode == "diff":
        in_specs += [_resident(lam_params.shape), _resident((1, LANES))]
        args += [lam_params, subln.reshape(1, LANES)]
    return pl.pallas_call(
        functools.partial(_attn_kernel, blk=blk, mode=mode, lam_init=lam_init),
        grid=(b, groups, t // blk),
        in_specs=in_specs,
        out_specs=pl.BlockSpec((None, blk, LANES), lambda i, g, j: (i, j, g)),
        out_shape=jax.ShapeDtypeStruct((b, t, groups * LANES), out_dtype),
        scratch_shapes=[pltpu.VMEM((2, blk, 1), F32), pltpu.VMEM((2, blk, 1), F32),
                        pltpu.VMEM((2, blk, LANES), F32)],
        compiler_params=_params("parallel", "parallel", "arbitrary"),
        name="attn_" + mode,
    )(*args)


def _out_kernel(h_ref, ya_ref, yb_ref, oc_ref, cn_ref, w_ref, o_ref, *, splits):
    yc = _rmsnorm(oc_ref[...], cn_ref[...]).astype(BF16)
    a_hi, b_hi = splits
    acc = jnp.dot(ya_ref[...], w_ref[:a_hi, :], preferred_element_type=F32)
    acc += jnp.dot(yb_ref[...], w_ref[a_hi:b_hi, :], preferred_element_type=F32)
    acc += jnp.dot(yc, w_ref[b_hi:, :], preferred_element_type=F32)
    o_ref[...] = h_ref[...] + acc


def _mixer_out(h, y_a, y_b, o_c, c_norm, w_out, *, row_tile=512):
    n, d = h.shape
    wa, wb, wc = y_a.shape[1], y_b.shape[1], o_c.shape[1]
    assert n % row_tile == 0 and wa + wb + wc == w_out.shape[0]
    rows = lambda width: pl.BlockSpec((row_tile, width), lambda i: (i, 0))
    return pl.pallas_call(
        functools.partial(_out_kernel, splits=(wa, wa + wb)),
        grid=(n // row_tile,),
        in_specs=[rows(d), rows(wa), rows(wb), rows(wc), _resident((1, wc)), _resident(w_out.shape)],
        out_specs=rows(d),
        out_shape=jax.ShapeDtypeStruct((n, d), F32),
        compiler_params=_params("parallel"),
        name="mixer_out",
    )(h, y_a, y_b, o_c, c_norm.reshape(1, wc), w_out)


def kernel(x, meta_tokens, ffn1_norm, ffn1_in, ffn1_out, mix_norm, w_in, conv_w, conv_b, lru_wa, lru_ba, lru_wx, lru_bx, lru_lambda, lru_out_norm, lam_q1, lam_k1, lam_q2, lam_k2, diff_subln, q_norm, w_uq, kv_norm, w_ukv, mla_out_norm, w_out, ffn2_norm, ffn2_in, ffn2_out, final_norm):
    b, s, d = x.shape
    depth = ffn1_in.shape[0]
    t_real = N_META + s
    t = -(-t_real // Q_BLOCK) * Q_BLOCK
    meta = jnp.broadcast_to(meta_tokens[None].astype(x.dtype), (b, N_META, d))
    h = jnp.concatenate([meta, x, jnp.zeros((b, t - t_real, d), x.dtype)], axis=1).reshape(b * t, d)
    tables = _rope_tables(t)
    seq_tile = t // 4 if (t // 4) % (2 * SUBLANES) == 0 else Q_BLOCK
    attn_blk = 3 * Q_BLOCK if t % (3 * Q_BLOCK) == 0 else Q_BLOCK

    for l in range(depth):
        h = _ffn(h, ffn1_norm[l], ffn1_in[l].astype(BF16), ffn1_out[l].astype(BF16))
        g_a, u_a, q_d, k_d, v_d, q_c, k_c, v_c = _mixer_proj(
            h.reshape(b, t, d), mix_norm[l], w_in[l], q_norm[l], w_uq[l], kv_norm[l], w_ukv[l], tables,
            row_tile=seq_tile)
        y_a = _lru(g_a, u_a, conv_w[l], conv_b[l], lru_wa[l], lru_ba[l], lru_wx[l], lru_bx[l], lru_lambda[l],
                   lru_out_norm[l], chunk=seq_tile)
        lam_init = 0.8 - 0.6 * math.exp(-0.3 * l)
        lam_params = jnp.stack([lam_q1[l], lam_k1[l], lam_q2[l], lam_k2[l]]).astype(F32)
        y_b = _attention(q_d, k_d, v_d, mode="diff", blk=attn_blk, lam_params=lam_params, subln=diff_subln[l],
                         lam_init=lam_init)
        o_c = _attention(q_c, k_c, v_c, mode="pair", blk=attn_blk, out_dtype=F32)
        n = b * t
        h = _mixer_out(h, y_a.reshape(n, -1), y_b.reshape(n, -1), o_c.reshape(n, -1), mla_out_norm[l],
                       w_out[l].astype(BF16))
        h = _ffn(h, ffn2_norm[l], ffn2_in[l].astype(BF16), ffn2_out[l].astype(BF16),
                 final_gain=final_norm if l == depth - 1 else None)
    return h.reshape(b, t, d)[:, N_META:N_META + s]
```

```python
import functools
import math

import jax
import jax.numpy as jnp
from jax import lax
from jax.experimental import pallas as pl
from jax.experimental.pallas import tpu as pltpu

F32 = jnp.float32
BF16 = jnp.bfloat16

N_META = 16
NORM_EPS = 1e-6
CONV_WIDTH = 4
LRU_C = 8.0
DIFF_QK_DIM = 64
MLA_HEADS = 4
MLA_NOPE_DIM = 64
MLA_ROPE_DIM = 32
MLA_V_DIM = 64
ROPE_THETA = 10000.0
LOG2_E = math.log2(math.e)

LANES = 128
SUBLANES = 8
MXU_DIM = 256
SEQ_BLOCK = MXU_DIM
VMEM_LIMIT_BYTES = 56 * 1024 * 1024

_NT = (((1,), (1,)), ((), ()))


def _params(*semantics):
    return pltpu.CompilerParams(dimension_semantics=semantics, vmem_limit_bytes=VMEM_LIMIT_BYTES)


def _resident(shape):
    zeros = (0,) * len(shape)
    return pl.BlockSpec(shape, lambda *_: zeros, pipeline_mode=pl.Buffered(1))


def _rmsnorm(x, gain):
    return x * lax.rsqrt(jnp.mean(x * x, axis=-1, keepdims=True) + NORM_EPS) * gain


def _ffn_kernel(h_ref, gain_ref, w_in_ref, w_out_ref, *rest, d_ff, ff_chunk, final_norm):
    if final_norm:
        final_gain_ref, o_ref, acc_ref = rest
    else:
        o_ref, acc_ref = rest
    x = h_ref[...]
    xn = _rmsnorm(x, gain_ref[...]).astype(BF16)
    for j in range(d_ff // ff_chunk):
        lo = j * ff_chunk
        gate = jnp.dot(xn, w_in_ref[:, lo:lo + ff_chunk], preferred_element_type=F32)
        up = jnp.dot(xn, w_in_ref[:, d_ff + lo:d_ff + lo + ff_chunk], preferred_element_type=F32)
        act = (jax.nn.silu(gate) * up).astype(BF16)
        part = jnp.dot(act, w_out_ref[lo:lo + ff_chunk, :], preferred_element_type=F32)
        if j == 0:
            acc_ref[...] = part
        else:
            acc_ref[...] += part
    y = x + 0.5 * acc_ref[...]
    if final_norm:
        y = _rmsnorm(y, final_gain_ref[...])
    o_ref[...] = y


def _ffn(h, gain, w_in, w_out, final_gain=None, *, row_tile=512, ff_chunk=256):
    n, d = h.shape
    d_ff = w_out.shape[0]
    assert n % row_tile == 0 and d_ff % ff_chunk == 0
    final_norm = final_gain is not None
    in_specs = [pl.BlockSpec((row_tile, d), lambda i: (i, 0)), _resident((1, d)),
                _resident(w_in.shape), _resident(w_out.shape)]
    args = [h, gain.reshape(1, d), w_in, w_out]
    if final_norm:
        in_specs.append(_resident((1, d)))
        args.append(final_gain.reshape(1, d))
    return pl.pallas_call(
        functools.partial(_ffn_kernel, d_ff=d_ff, ff_chunk=ff_chunk, final_norm=final_norm),
        grid=(n // row_tile,),
        in_specs=in_specs,
        out_specs=pl.BlockSpec((row_tile, d), lambda i: (i, 0)),
        out_shape=jax.ShapeDtypeStruct((n, d), F32),
        scratch_shapes=[pltpu.VMEM((row_tile, d), F32)],
        compiler_params=_params("parallel"),
        name="ffn",
    )(*args)


def _rope(x, cos, sin_lo, sin_hi):
    half = MLA_ROPE_DIM // 2
    return x * cos + pltpu.roll(x, LANES - half, 1) * sin_lo + pltpu.roll(x, half, 1) * sin_hi


def _proj_kernel(h_ref, gain_ref, w_ref, wvt_ref, qn_ref, wuq_ref, kvn_ref, wuk_ref, wuvt_ref,
                 cos_ref, slo_ref, shi_ref,
                 g_ref, u_ref, qd_ref, kd_ref, vdt_ref, qc_ref, kc_ref, vct_ref, *, offs):
    xn = _rmsnorm(h_ref[...], gain_ref[...]).astype(BF16)

    def proj(name):
        lo, hi = offs[name]
        return jnp.dot(xn, w_ref[:, lo:hi], preferred_element_type=F32)

    g_ref[...] = proj("g_a")
    u_ref[...] = proj("u_a")
    qd_ref[...] = (proj("q_d") * (DIFF_QK_DIM ** -0.5 * LOG2_E)).astype(BF16)
    kd_ref[...] = proj("k_d").astype(BF16)
    vdt_ref[...] = lax.dot_general(wvt_ref[...], xn, _NT, preferred_element_type=F32).astype(BF16)

    cos, slo, shi = cos_ref[...], slo_ref[...], shi_ref[...]
    cq = _rmsnorm(proj("c_q"), qn_ref[...]).astype(BF16)
    q = jnp.dot(cq, wuq_ref[...], preferred_element_type=F32)
    ckv = _rmsnorm(proj("c_kv"), kvn_ref[...]).astype(BF16)
    k_nope = jnp.dot(ckv, wuk_ref[...], preferred_element_type=F32)
    vct_ref[...] = lax.dot_general(wuvt_ref[...], ckv, _NT, preferred_element_type=F32).astype(BF16)
    k_rope = _rope(proj("k_r"), cos, slo, shi)
    q_scale = (MLA_NOPE_DIM + MLA_ROPE_DIM) ** -0.5 * LOG2_E
    for hd in range(MLA_HEADS):
        sl = slice(hd * LANES, (hd + 1) * LANES)
        qc_ref[:, sl] = (_rope(q[:, sl], cos, slo, shi) * q_scale).astype(BF16)
        kc_ref[:, sl] = (k_nope[:, sl] + k_rope).astype(BF16)


def _rope_tables(t_len):
    half = MLA_ROPE_DIM // 2
    inv_freq = ROPE_THETA ** (-jnp.arange(half, dtype=F32) / half)
    ang = jnp.arange(t_len, dtype=jnp.int32).astype(F32)[:, None] * inv_freq[None, :]
    cos, sin = jnp.cos(ang), jnp.sin(ang)
    zeros = jnp.zeros((t_len, half), F32)
    ones = jnp.ones((t_len, MLA_NOPE_DIM), F32)
    tail = jnp.zeros((t_len, LANES - MLA_NOPE_DIM - MLA_ROPE_DIM), F32)
    cos_t = jnp.concatenate([ones, cos, cos, tail], axis=1)
    sin_lo = jnp.concatenate([0 * ones, -sin, zeros, tail], axis=1)
    sin_hi = jnp.concatenate([0 * ones, zeros, sin, tail], axis=1)
    return cos_t, sin_lo, sin_hi


def _pad_heads(w, heads, width, lo, hi):
    k = w.shape[0]
    w = w.reshape(k, heads, width)[:, :, lo:hi]
    w = jnp.pad(w, ((0, 0), (0, 0), (0, LANES - (hi - lo))))
    return w.reshape(k, heads * LANES)


def _mixer_proj(h3, gain, w_in, q_norm, w_uq, kv_norm, w_ukv, tables):
    b, t, d = h3.shape
    blk = SEQ_BLOCK
    assert t % blk == 0
    widths = dict(g_a=256, u_a=256, q_d=512, k_d=512, v_d=512, c_q=w_uq.shape[0], c_kv=w_ukv.shape[0],
                  k_r=MLA_ROPE_DIM)
    offs, lo = {}, 0
    for name, wd in widths.items():
        offs[name] = (lo, lo + wd)
        lo += wd
    assert lo == w_in.shape[1]
    kr_lo, kr_hi = offs["k_r"]
    w_kr = jnp.pad(w_in[:, kr_lo:kr_hi], ((0, 0), (MLA_NOPE_DIM, LANES - MLA_NOPE_DIM - MLA_ROPE_DIM)))
    w_all = jnp.concatenate([w_in[:, :kr_lo], w_kr], axis=1).astype(BF16)
    offs["k_r"] = (kr_lo, kr_lo + LANES)
    v_lo, v_hi = offs["v_d"]
    w_vt = w_in[:, v_lo:v_hi].T.astype(BF16)
    qk = MLA_NOPE_DIM + MLA_ROPE_DIM
    w_uq_p = _pad_heads(w_uq, MLA_HEADS, qk, 0, qk).astype(BF16)
    w_uk_p = _pad_heads(w_ukv, MLA_HEADS, MLA_NOPE_DIM + MLA_V_DIM, 0, MLA_NOPE_DIM).astype(BF16)
    w_uv = w_ukv.reshape(-1, MLA_HEADS, MLA_NOPE_DIM + MLA_V_DIM)[:, :, MLA_NOPE_DIM:]
    w_uvt = w_uv.reshape(-1, MLA_HEADS * MLA_V_DIM).T.astype(BF16)

    def rows(width, dtype):
        return (pl.BlockSpec((None, blk, width), lambda i, j: (i, j, 0)),
                jax.ShapeDtypeStruct((b, t, width), dtype))

    def cols(width):
        return (pl.BlockSpec((None, None, width, blk), lambda i, j: (i, j, 0, 0)),
                jax.ShapeDtypeStruct((b, t // blk, width, blk), BF16))

    outs = [rows(256, F32), rows(256, F32), rows(512, BF16), rows(512, BF16), cols(512),
            rows(MLA_HEADS * LANES, BF16), rows(MLA_HEADS * LANES, BF16), cols(MLA_HEADS * MLA_V_DIM)]
    table_spec = pl.BlockSpec((blk, LANES), lambda i, j: (j, 0))
    return pl.pallas_call(
        functools.partial(_proj_kernel, offs=offs),
        grid=(b, t // blk),
        in_specs=[pl.BlockSpec((None, blk, d), lambda i, j: (i, j, 0)), _resident((1, d)),
                  _resident(w_all.shape), _resident(w_vt.shape), _resident((1, w_uq.shape[0])),
                  _resident(w_uq_p.shape), _resident((1, w_ukv.shape[0])), _resident(w_uk_p.shape),
                  _resident(w_uvt.shape), table_spec, table_spec, table_spec],
        out_specs=[o[0] for o in outs],
        out_shape=[o[1] for o in outs],
        compiler_params=_params("parallel", "parallel"),
        name="mixer_proj",
    )(h3, gain.reshape(1, d), w_all, w_vt, q_norm.reshape(1, -1), w_uq_p, kv_norm.reshape(1, -1), w_uk_p, w_uvt,
      *tables)


def _lru_kernel(g_ref, u_ref, cw_ref, cb_ref, wa_ref, ba_ref, wx_ref, bx_ref, lam_ref, on_ref,
                y_ref, halo_ref, carry_ref, a_scr, b_scr, h_scr, *, chunk):
    @pl.when(pl.program_id(1) == 0)
    def _():
        halo_ref[...] = jnp.zeros_like(halo_ref)
        carry_ref[...] = jnp.zeros_like(carry_ref)

    u = u_ref[...]
    ext = jnp.concatenate([halo_ref[...], u], axis=0)
    halo_ref[...] = u[chunk - SUBLANES:, :]
    conv = cb_ref[...] + cw_ref[CONV_WIDTH - 1:CONV_WIDTH, :] * u
    for j in range(CONV_WIDTH - 1):
        shifted = pltpu.roll(ext, CONV_WIDTH - 1 - j, 0)[SUBLANES:, :]
        conv = conv + cw_ref[j:j + 1, :] * shifted

    ub = conv.astype(BF16)
    r = jax.nn.sigmoid(jnp.dot(ub, wa_ref[...], preferred_element_type=F32) + ba_ref[...])
    i = jax.nn.sigmoid(jnp.dot(ub, wx_ref[...], preferred_element_type=F32) + bx_ref[...])
    neg_lam = -lam_ref[...]
    softplus = jnp.maximum(neg_lam, 0.0) + jnp.log1p(jnp.exp(-jnp.abs(neg_lam)))
    log_a = -LRU_C * r * softplus
    a = jnp.exp(log_a)
    b = jnp.sqrt(-jnp.tanh(log_a) * (a * a + 1.0)) * (i * conv)

    row = jnp.bitwise_and(lax.broadcasted_iota(jnp.int32, a.shape, 0), SUBLANES - 1)
    for s in (1, 2, 4):
        keep = row >= s
        b = jnp.where(keep, a * pltpu.roll(b, s, 0) + b, b)
        a = jnp.where(keep, a * pltpu.roll(a, s, 0), a)
    a_scr[...] = a
    b_scr[...] = b

    def group(gi, carry):
        sl = pl.ds(pl.multiple_of(gi * SUBLANES, SUBLANES), SUBLANES)
        h = a_scr[sl, :] * carry + b_scr[sl, :]
        h_scr[sl, :] = h
        return jnp.broadcast_to(h[SUBLANES - 1:SUBLANES, :], h.shape)

    carry_ref[...] = lax.fori_loop(0, chunk // SUBLANES, group, carry_ref[...])
    y = h_scr[...] * jax.nn.gelu(g_ref[...])
    y_ref[...] = _rmsnorm(y, on_ref[...]).astype(BF16)


def _block_diag(w):
    n, c, d = w.shape
    eye = jnp.eye(n, dtype=w.dtype)
    return (w[:, :, None, :] * eye[:, None, :, None]).reshape(n * c, n * d)


def _lru(g, u, conv_w, conv_b, wa, ba, wx, bx, lam, out_norm, *, chunk):
    b, t, c = u.shape
    assert t % chunk == 0 and chunk % (2 * SUBLANES) == 0
    row = lambda v: v.reshape(1, c)
    seq = pl.BlockSpec((None, chunk, c), lambda i, j: (i, j, 0))
    return pl.pallas_call(
        functools.partial(_lru_kernel, chunk=chunk),
        grid=(b, t // chunk),
        in_specs=[seq, seq, _resident((CONV_WIDTH, c)), _resident((1, c)), _resident((c, c)), _resident((1, c)),
                  _resident((c, c)), _resident((1, c)), _resident((1, c)), _resident((1, c))],
        out_specs=seq,
        out_shape=jax.ShapeDtypeStruct((b, t, c), BF16),
        scratch_shapes=[pltpu.VMEM((SUBLANES, c), F32), pltpu.VMEM((SUBLANES, c), F32),
                        pltpu.VMEM((chunk, c), F32), pltpu.VMEM((chunk, c), F32), pltpu.VMEM((chunk, c), F32)],
        compiler_params=_params("parallel", "arbitrary"),
        name="rg_lru",
    )(g, u, conv_w, row(conv_b), _block_diag(wa).astype(BF16), row(ba), _block_diag(wx).astype(BF16), row(bx),
      row(lam), row(out_norm))


def _attn_kernel(q_ref, k_ref, vt_ref, *rest, blk, mode, groups, lam_init):
    if mode == "diff":
        lam_ref, subln_ref, o_ref, m_scr, l_scr, acc_scr = rest
    else:
        o_ref, m_scr, l_scr, acc_scr = rest
    qi = pl.program_id(2)
    width = LANES if mode == "diff" else 2 * LANES
    chains = [(g, c) for g in range(groups) for c in range(2)]
    qs = []
    for g in range(groups):
        if mode == "diff":
            q = q_ref[:, g * width:(g + 1) * width]
            low = lax.broadcasted_iota(jnp.int32, q.shape, 1) < (LANES // 2)
            zero = jnp.zeros_like(q)
            qs += [jnp.where(low, q, zero), jnp.where(low, zero, q)]
        else:
            qs += [q_ref[:, g * width:g * width + LANES], q_ref[:, g * width + LANES:(g + 1) * width]]

    m_scr[...] = jnp.full(m_scr.shape, -jnp.inf, F32)
    l_scr[...] = jnp.zeros_like(l_scr)
    acc_scr[...] = jnp.zeros_like(acc_scr)

    def scores(kj):
        rows = pl.ds(pl.multiple_of(kj * blk, blk), blk)
        out = []
        for i, (g, c) in enumerate(chains):
            lo = g * width + (0 if mode == "diff" else c * LANES)
            out.append(lax.dot_general(k_ref[rows, lo:lo + LANES], qs[i], _NT, preferred_element_type=F32))
        return tuple(out)

    def accumulate(kj, ss, masked):
        ps = []
        for i in range(len(chains)):
            s = ss[i]
            if masked:
                key = lax.broadcasted_iota(jnp.int32, s.shape, 0)
                qry = lax.broadcasted_iota(jnp.int32, s.shape, 1)
                s = jnp.where(key <= qry, s, -jnp.inf)
            m_old = m_scr[i]
            m_new = jnp.maximum(m_old, jnp.max(s, axis=0, keepdims=True))
            alpha = jnp.exp2(m_old - m_new)
            p = jnp.exp2(s - m_new)
            l_scr[i] = alpha * l_scr[i] + jnp.sum(p, axis=0, keepdims=True)
            m_scr[i] = m_new
            ps.append((alpha, p.astype(BF16)))
        for i, (g, c) in enumerate(chains):
            alpha, p = ps[i]
            vt = vt_ref[kj, g * LANES:(g + 1) * LANES, :]
            acc_scr[i] = alpha * acc_scr[i] + jnp.dot(vt, p, preferred_element_type=F32)

    def body(kj, ss):
        nxt = scores(kj + 1)
        accumulate(kj, ss, False)
        return nxt

    pairs = qi // 2
    ss = lax.fori_loop(0, pairs, lambda i, ss: body(2 * i + 1, body(2 * i, ss)), scores(0))
    ss = lax.fori_loop(2 * pairs, qi, body, ss)
    accumulate(qi, ss, True)

    if mode == "diff":
        p_ = lam_ref[...]
        lam = (jnp.exp(jnp.sum(p_[0:1] * p_[1:2], axis=-1, keepdims=True))
               - jnp.exp(jnp.sum(p_[2:3] * p_[3:4], axis=-1, keepdims=True)) + lam_init)
    for g in range(groups):
        o_a = acc_scr[2 * g] * (1.0 / l_scr[2 * g])
        o_b = acc_scr[2 * g + 1] * (1.0 / l_scr[2 * g + 1])
        cols = slice(g * LANES, (g + 1) * LANES)
        if mode == "diff":
            o = (o_a - lam * o_b).T
            o_ref[:, cols] = (_rmsnorm(o, subln_ref[...]) * (1.0 - lam_init)).astype(o_ref.dtype)
        else:
            top = lax.broadcasted_iota(jnp.int32, o_a.shape, 0) < (LANES // 2)
            o_ref[:, cols] = jnp.where(top, o_a, o_b).T.astype(o_ref.dtype)


def _attention(q, k, vt, *, mode, lam_params=None, subln=None, lam_init=0.0, out_dtype=BF16, groups_per_step=2):
    b, t, _ = q.shape
    blk = vt.shape[3]
    width = LANES if mode == "diff" else 2 * LANES
    total = vt.shape[2] // LANES
    groups = min(total, groups_per_step)
    assert t % blk == 0 and q.shape[2] == total * width and total % groups == 0
    in_specs = [pl.BlockSpec((None, blk, groups * width), lambda i, g, j: (i, j, g)),
                pl.BlockSpec((None, t, groups * width), lambda i, g, j: (i, 0, g)),
                pl.BlockSpec((None, t // blk, groups * LANES, blk), lambda i, g, j: (i, 0, g, 0))]
    args = [q, k, vt]
    if mode == "diff":
        in_specs += [_resident(lam_params.shape), _resident((1, LANES))]
        args += [lam_params, subln.reshape(1, LANES)]
    chains = 2 * groups
    return pl.pallas_call(
        functools.partial(_attn_kernel, blk=blk, mode=mode, groups=groups, lam_init=lam_init),
        grid=(b, total // groups, t // blk),
        in_specs=in_specs,
        out_specs=pl.BlockSpec((None, blk, groups * LANES), lambda i, g, j: (i, j, g)),
        out_shape=jax.ShapeDtypeStruct((b, t, total * LANES), out_dtype),
        scratch_shapes=[pltpu.VMEM((chains, 1, blk), F32), pltpu.VMEM((chains, 1, blk), F32),
                        pltpu.VMEM((chains, LANES, blk), F32)],
        compiler_params=_params("parallel", "parallel", "arbitrary"),
        name="attn_" + mode,
    )(*args)


def _out_kernel(h_ref, ya_ref, yb_ref, oc_ref, cn_ref, w_ref, o_ref, *, splits):
    yc = _rmsnorm(oc_ref[...], cn_ref[...]).astype(BF16)
    a_hi, b_hi = splits
    acc = jnp.dot(ya_ref[...], w_ref[:a_hi, :], preferred_element_type=F32)
    acc += jnp.dot(yb_ref[...], w_ref[a_hi:b_hi, :], preferred_element_type=F32)
    acc += jnp.dot(yc, w_ref[b_hi:, :], preferred_element_type=F32)
    o_ref[...] = h_ref[...] + acc


def _mixer_out(h, y_a, y_b, o_c, c_norm, w_out, *, row_tile=512):
    n, d = h.shape
    wa, wb, wc = y_a.shape[1], y_b.shape[1], o_c.shape[1]
    assert n % row_tile == 0 and wa + wb + wc == w_out.shape[0]
    rows = lambda width: pl.BlockSpec((row_tile, width), lambda i: (i, 0))
    return pl.pallas_call(
        functools.partial(_out_kernel, splits=(wa, wa + wb)),
        grid=(n // row_tile,),
        in_specs=[rows(d), rows(wa), rows(wb), rows(wc), _resident((1, wc)), _resident(w_out.shape)],
        out_specs=rows(d),
        out_shape=jax.ShapeDtypeStruct((n, d), F32),
        compiler_params=_params("parallel"),
        name="mixer_out",
    )(h, y_a, y_b, o_c, c_norm.reshape(1, wc), w_out)


def kernel(x, meta_tokens, ffn1_norm, ffn1_in, ffn1_out, mix_norm, w_in, conv_w, conv_b, lru_wa, lru_ba, lru_wx, lru_bx, lru_lambda, lru_out_norm, lam_q1, lam_k1, lam_q2, lam_k2, diff_subln, q_norm, w_uq, kv_norm, w_ukv, mla_out_norm, w_out, ffn2_norm, ffn2_in, ffn2_out, final_norm):
    b, s, d = x.shape
    depth = ffn1_in.shape[0]
    t_real = N_META + s
    t = -(-t_real // SEQ_BLOCK) * SEQ_BLOCK
    meta = jnp.broadcast_to(meta_tokens[None].astype(x.dtype), (b, N_META, d))
    h = jnp.concatenate([meta, x, jnp.zeros((b, t - t_real, d), x.dtype)], axis=1).reshape(b * t, d)
    n = b * t
    tables = _rope_tables(t)
    lru_chunk = t // 4 if (t // 4) % (2 * SUBLANES) == 0 else SEQ_BLOCK

    for l in range(depth):
        h = _ffn(h, ffn1_norm[l], ffn1_in[l].astype(BF16), ffn1_out[l].astype(BF16))
        g_a, u_a, q_d, k_d, vt_d, q_c, k_c, vt_c = _mixer_proj(
            h.reshape(b, t, d), mix_norm[l], w_in[l], q_norm[l], w_uq[l], kv_norm[l], w_ukv[l], tables)
        y_a = _lru(g_a, u_a, conv_w[l], conv_b[l], lru_wa[l], lru_ba[l], lru_wx[l], lru_bx[l], lru_lambda[l],
                   lru_out_norm[l], chunk=lru_chunk)
        lam_init = 0.8 - 0.6 * math.exp(-0.3 * l)
        lam_params = jnp.stack([lam_q1[l], lam_k1[l], lam_q2[l], lam_k2[l]]).astype(F32)
        y_b = _attention(q_d, k_d, vt_d, mode="diff", lam_params=lam_params, subln=diff_subln[l],
                         lam_init=lam_init)
        o_c = _attention(q_c, k_c, vt_c, mode="pair", out_dtype=F32)
        h = _mixer_out(h, y_a.reshape(n, -1), y_b.reshape(n, -1), o_c.reshape(n, -1), mla_out_norm[l],
                       w_out[l].astype(BF16))
        h = _ffn(h, ffn2_norm[l], ffn2_in[l].astype(BF16), ffn2_out[l].astype(BF16),
                 final_gain=final_norm if l == depth - 1 else None)
    return h.reshape(b, t, d)[:, N_META:N_META + s]
```

```python
import functools
import math

import jax
import jax.numpy as jnp
from jax import lax
from jax.experimental import pallas as pl
from jax.experimental.pallas import tpu as pltpu

F32 = jnp.float32
BF16 = jnp.bfloat16

N_META = 16
NORM_EPS = 1e-6
CONV_WIDTH = 4
LRU_C = 8.0
DIFF_QK_DIM = 64
MLA_HEADS = 4
MLA_NOPE_DIM = 64
MLA_ROPE_DIM = 32
MLA_V_DIM = 64
ROPE_THETA = 10000.0
LOG2_E = math.log2(math.e)

LANES = 128
SUBLANES = 8
MXU_DIM = 256
SEQ_BLOCK = MXU_DIM
VMEM_LIMIT_BYTES = 56 * 1024 * 1024

_NT = (((1,), (1,)), ((), ()))


def _params(*semantics):
    return pltpu.CompilerParams(dimension_semantics=semantics, vmem_limit_bytes=VMEM_LIMIT_BYTES)


def _resident(shape):
    zeros = (0,) * len(shape)
    return pl.BlockSpec(shape, lambda *_: zeros, pipeline_mode=pl.Buffered(1))


def _rmsnorm(x, gain):
    return x * lax.rsqrt(jnp.mean(x * x, axis=-1, keepdims=True) + NORM_EPS) * gain


def _ffn_kernel(h_ref, gain_ref, w_in_ref, w_out_ref, *rest, d_ff, ff_chunk, final_norm):
    if final_norm:
        final_gain_ref, o_ref, acc_ref = rest
    else:
        o_ref, acc_ref = rest
    x = h_ref[...]
    xn = _rmsnorm(x, gain_ref[...]).astype(BF16)
    for j in range(d_ff // ff_chunk):
        lo = j * ff_chunk
        gate = jnp.dot(xn, w_in_ref[:, lo:lo + ff_chunk], preferred_element_type=F32)
        up = jnp.dot(xn, w_in_ref[:, d_ff + lo:d_ff + lo + ff_chunk], preferred_element_type=F32)
        act = (jax.nn.silu(gate) * up).astype(BF16)
        part = jnp.dot(act, w_out_ref[lo:lo + ff_chunk, :], preferred_element_type=F32)
        if j == 0:
            acc_ref[...] = part
        else:
            acc_ref[...] += part
    y = x + 0.5 * acc_ref[...]
    if final_norm:
        y = _rmsnorm(y, final_gain_ref[...])
    o_ref[...] = y


def _ffn(h, gain, w_in, w_out, final_gain=None, *, row_tile=512, ff_chunk=256):
    n, d = h.shape
    d_ff = w_out.shape[0]
    assert n % row_tile == 0 and d_ff % ff_chunk == 0
    final_norm = final_gain is not None
    in_specs = [pl.BlockSpec((row_tile, d), lambda i: (i, 0)), _resident((1, d)),
                _resident(w_in.shape), _resident(w_out.shape)]
    args = [h, gain.reshape(1, d), w_in, w_out]
    if final_norm:
        in_specs.append(_resident((1, d)))
        args.append(final_gain.reshape(1, d))
    return pl.pallas_call(
        functools.partial(_ffn_kernel, d_ff=d_ff, ff_chunk=ff_chunk, final_norm=final_norm),
        grid=(n // row_tile,),
        in_specs=in_specs,
        out_specs=pl.BlockSpec((row_tile, d), lambda i: (i, 0)),
        out_shape=jax.ShapeDtypeStruct((n, d), F32),
        scratch_shapes=[pltpu.VMEM((row_tile, d), F32)],
        compiler_params=_params("parallel"),
        name="ffn",
    )(*args)


def _rope(x, cos, sin_lo, sin_hi):
    half = MLA_ROPE_DIM // 2
    return x * cos + pltpu.roll(x, LANES - half, 1) * sin_lo + pltpu.roll(x, half, 1) * sin_hi


def _proj_kernel(h_ref, gain_ref, w_ref, wvt_ref, qn_ref, wuq_ref, kvn_ref, wuk_ref, wuvt_ref,
                 cos_ref, slo_ref, shi_ref,
                 g_ref, u_ref, qd_ref, kd_ref, vdt_ref, qc_ref, kc_ref, vct_ref, *, offs):
    xn = _rmsnorm(h_ref[...], gain_ref[...]).astype(BF16)

    def proj(name):
        lo, hi = offs[name]
        return jnp.dot(xn, w_ref[:, lo:hi], preferred_element_type=F32)

    g_ref[...] = proj("g_a")
    u_ref[...] = proj("u_a")
    qd_ref[...] = (proj("q_d") * (DIFF_QK_DIM ** -0.5 * LOG2_E)).astype(BF16)
    kd_ref[...] = proj("k_d").astype(BF16)
    vdt_ref[...] = lax.dot_general(wvt_ref[...], xn, _NT, preferred_element_type=F32).astype(BF16)

    cos, slo, shi = cos_ref[...], slo_ref[...], shi_ref[...]
    cq = _rmsnorm(proj("c_q"), qn_ref[...]).astype(BF16)
    q = jnp.dot(cq, wuq_ref[...], preferred_element_type=F32)
    ckv = _rmsnorm(proj("c_kv"), kvn_ref[...]).astype(BF16)
    k_nope = jnp.dot(ckv, wuk_ref[...], preferred_element_type=F32)
    vct_ref[...] = lax.dot_general(wuvt_ref[...], ckv, _NT, preferred_element_type=F32).astype(BF16)
    k_rope = _rope(proj("k_r"), cos, slo, shi)
    q_scale = (MLA_NOPE_DIM + MLA_ROPE_DIM) ** -0.5 * LOG2_E
    for hd in range(MLA_HEADS):
        sl = slice(hd * LANES, (hd + 1) * LANES)
        qc_ref[:, sl] = (_rope(q[:, sl], cos, slo, shi) * q_scale).astype(BF16)
        kc_ref[:, sl] = (k_nope[:, sl] + k_rope).astype(BF16)


def _rope_tables(t_len):
    half = MLA_ROPE_DIM // 2
    inv_freq = ROPE_THETA ** (-jnp.arange(half, dtype=F32) / half)
    ang = jnp.arange(t_len, dtype=jnp.int32).astype(F32)[:, None] * inv_freq[None, :]
    cos, sin = jnp.cos(ang), jnp.sin(ang)
    zeros = jnp.zeros((t_len, half), F32)
    ones = jnp.ones((t_len, MLA_NOPE_DIM), F32)
    tail = jnp.zeros((t_len, LANES - MLA_NOPE_DIM - MLA_ROPE_DIM), F32)
    cos_t = jnp.concatenate([ones, cos, cos, tail], axis=1)
    sin_lo = jnp.concatenate([0 * ones, -sin, zeros, tail], axis=1)
    sin_hi = jnp.concatenate([0 * ones, zeros, sin, tail], axis=1)
    return cos_t, sin_lo, sin_hi


def _pad_heads(w, heads, width, lo, hi):
    k = w.shape[0]
    w = w.reshape(k, heads, width)[:, :, lo:hi]
    w = jnp.pad(w, ((0, 0), (0, 0), (0, LANES - (hi - lo))))
    return w.reshape(k, heads * LANES)


def _mixer_proj(h3, gain, w_in, q_norm, w_uq, kv_norm, w_ukv, tables):
    b, t, d = h3.shape
    blk = SEQ_BLOCK
    assert t % blk == 0
    widths = dict(g_a=256, u_a=256, q_d=512, k_d=512, v_d=512, c_q=w_uq.shape[0], c_kv=w_ukv.shape[0],
                  k_r=MLA_ROPE_DIM)
    offs, lo = {}, 0
    for name, wd in widths.items():
        offs[name] = (lo, lo + wd)
        lo += wd
    assert lo == w_in.shape[1]
    kr_lo, kr_hi = offs["k_r"]
    w_kr = jnp.pad(w_in[:, kr_lo:kr_hi], ((0, 0), (MLA_NOPE_DIM, LANES - MLA_NOPE_DIM - MLA_ROPE_DIM)))
    w_all = jnp.concatenate([w_in[:, :kr_lo], w_kr], axis=1).astype(BF16)
    offs["k_r"] = (kr_lo, kr_lo + LANES)
    v_lo, v_hi = offs["v_d"]
    w_vt = w_in[:, v_lo:v_hi].T.astype(BF16)
    qk = MLA_NOPE_DIM + MLA_ROPE_DIM
    w_uq_p = _pad_heads(w_uq, MLA_HEADS, qk, 0, qk).astype(BF16)
    w_uk_p = _pad_heads(w_ukv, MLA_HEADS, MLA_NOPE_DIM + MLA_V_DIM, 0, MLA_NOPE_DIM).astype(BF16)
    w_uv = w_ukv.reshape(-1, MLA_HEADS, MLA_NOPE_DIM + MLA_V_DIM)[:, :, MLA_NOPE_DIM:]
    w_uvt = w_uv.reshape(-1, MLA_HEADS * MLA_V_DIM).T.astype(BF16)

    def rows(width, dtype):
        return (pl.BlockSpec((None, blk, width), lambda i, j: (i, j, 0)),
                jax.ShapeDtypeStruct((b, t, width), dtype))

    def cols(width):
        return (pl.BlockSpec((None, None, width, blk), lambda i, j: (i, j, 0, 0)),
                jax.ShapeDtypeStruct((b, t // blk, width, blk), BF16))

    outs = [rows(256, F32), rows(256, F32), rows(512, BF16), rows(512, BF16), cols(512),
            rows(MLA_HEADS * LANES, BF16), rows(MLA_HEADS * LANES, BF16), cols(MLA_HEADS * MLA_V_DIM)]
    table_spec = pl.BlockSpec((blk, LANES), lambda i, j: (j, 0))
    return pl.pallas_call(
        functools.partial(_proj_kernel, offs=offs),
        grid=(b, t // blk),
        in_specs=[pl.BlockSpec((None, blk, d), lambda i, j: (i, j, 0)), _resident((1, d)),
                  _resident(w_all.shape), _resident(w_vt.shape), _resident((1, w_uq.shape[0])),
                  _resident(w_uq_p.shape), _resident((1, w_ukv.shape[0])), _resident(w_uk_p.shape),
                  _resident(w_uvt.shape), table_spec, table_spec, table_spec],
        out_specs=[o[0] for o in outs],
        out_shape=[o[1] for o in outs],
        compiler_params=_params("parallel", "parallel"),
        name="mixer_proj",
    )(h3, gain.reshape(1, d), w_all, w_vt, q_norm.reshape(1, -1), w_uq_p, kv_norm.reshape(1, -1), w_uk_p, w_uvt,
      *tables)


def _lru_kernel(g_ref, u_ref, cw_ref, cb_ref, wa_ref, ba_ref, wx_ref, bx_ref, lam_ref, on_ref,
                y_ref, halo_ref, carry_ref, a_scr, b_scr, h_scr, *, chunk):
    @pl.when(pl.program_id(1) == 0)
    def _():
        halo_ref[...] = jnp.zeros_like(halo_ref)
        carry_ref[...] = jnp.zeros_like(carry_ref)

    u = u_ref[...]
    ext = jnp.concatenate([halo_ref[...], u], axis=0)
    halo_ref[...] = u[chunk - SUBLANES:, :]
    conv = cb_ref[...] + cw_ref[CONV_WIDTH - 1:CONV_WIDTH, :] * u
    for j in range(CONV_WIDTH - 1):
        shifted = pltpu.roll(ext, CONV_WIDTH - 1 - j, 0)[SUBLANES:, :]
        conv = conv + cw_ref[j:j + 1, :] * shifted

    ub = conv.astype(BF16)
    r = jax.nn.sigmoid(jnp.dot(ub, wa_ref[...], preferred_element_type=F32) + ba_ref[...])
    i = jax.nn.sigmoid(jnp.dot(ub, wx_ref[...], preferred_element_type=F32) + bx_ref[...])
    neg_lam = -lam_ref[...]
    softplus = jnp.maximum(neg_lam, 0.0) + jnp.log1p(jnp.exp(-jnp.abs(neg_lam)))
    log_a = -LRU_C * r * softplus
    a = jnp.exp(log_a)
    b = jnp.sqrt(-jnp.tanh(log_a) * (a * a + 1.0)) * (i * conv)

    row = jnp.bitwise_and(lax.broadcasted_iota(jnp.int32, a.shape, 0), SUBLANES - 1)
    for s in (1, 2, 4):
        keep = row >= s
        b = jnp.where(keep, a * pltpu.roll(b, s, 0) + b, b)
        a = jnp.where(keep, a * pltpu.roll(a, s, 0), a)
    a_scr[...] = a
    b_scr[...] = b

    def group(gi, carry):
        sl = pl.ds(pl.multiple_of(gi * SUBLANES, SUBLANES), SUBLANES)
        h = a_scr[sl, :] * carry + b_scr[sl, :]
        h_scr[sl, :] = h
        return jnp.broadcast_to(h[SUBLANES - 1:SUBLANES, :], h.shape)

    carry_ref[...] = lax.fori_loop(0, chunk // SUBLANES, group, carry_ref[...])
    y = h_scr[...] * jax.nn.gelu(g_ref[...])
    y_ref[...] = _rmsnorm(y, on_ref[...]).astype(BF16)


def _block_diag(w):
    n, c, d = w.shape
    eye = jnp.eye(n, dtype=w.dtype)
    return (w[:, :, None, :] * eye[:, None, :, None]).reshape(n * c, n * d)


def _lru(g, u, conv_w, conv_b, wa, ba, wx, bx, lam, out_norm, *, chunk):
    b, t, c = u.shape
    assert t % chunk == 0 and chunk % (2 * SUBLANES) == 0
    row = lambda v: v.reshape(1, c)
    seq = pl.BlockSpec((None, chunk, c), lambda i, j: (i, j, 0))
    return pl.pallas_call(
        functools.partial(_lru_kernel, chunk=chunk),
        grid=(b, t // chunk),
        in_specs=[seq, seq, _resident((CONV_WIDTH, c)), _resident((1, c)), _resident((c, c)), _resident((1, c)),
                  _resident((c, c)), _resident((1, c)), _resident((1, c)), _resident((1, c))],
        out_specs=seq,
        out_shape=jax.ShapeDtypeStruct((b, t, c), BF16),
        scratch_shapes=[pltpu.VMEM((SUBLANES, c), F32), pltpu.VMEM((SUBLANES, c), F32),
                        pltpu.VMEM((chunk, c), F32), pltpu.VMEM((chunk, c), F32), pltpu.VMEM((chunk, c), F32)],
        compiler_params=_params("parallel", "arbitrary"),
        name="rg_lru",
    )(g, u, conv_w, row(conv_b), _block_diag(wa).astype(BF16), row(ba), _block_diag(wx).astype(BF16), row(bx),
      row(lam), row(out_norm))


def _attn_kernel(q_ref, k_ref, vt_ref, *rest, blk, mode, groups, lam_init):
    if mode == "diff":
        lam_ref, subln_ref, o_ref, m_scr, l_scr, acc_scr, s_scr = rest
    else:
        o_ref, m_scr, l_scr, acc_scr, s_scr = rest
    qi = pl.program_id(2)
    width = LANES if mode == "diff" else 2 * LANES
    chains = [(g, c) for g in range(groups) for c in range(2)]
    qs = []
    for g in range(groups):
        if mode == "diff":
            q = q_ref[:, g * width:(g + 1) * width]
            low = lax.broadcasted_iota(jnp.int32, q.shape, 1) < (LANES // 2)
            zero = jnp.zeros_like(q)
            qs += [jnp.where(low, q, zero), jnp.where(low, zero, q)]
        else:
            qs += [q_ref[:, g * width:g * width + LANES], q_ref[:, g * width + LANES:(g + 1) * width]]

    def scores(kj):
        rows = pl.ds(pl.multiple_of(kj * blk, blk), blk)
        out = []
        for i, (g, c) in enumerate(chains):
            lo = g * width + (0 if mode == "diff" else c * LANES)
            out.append(lax.dot_general(k_ref[rows, lo:lo + LANES], qs[i], _NT, preferred_element_type=F32))
        return out

    def values(kj, g):
        return vt_ref[kj, g * LANES:(g + 1) * LANES, :]

    diag = scores(qi)
    for i, s in enumerate(scores(0)):
        s_scr[0, i] = s
    ps = []
    for i in range(len(chains)):
        key = lax.broadcasted_iota(jnp.int32, diag[i].shape, 0)
        qry = lax.broadcasted_iota(jnp.int32, diag[i].shape, 1)
        s = jnp.where(key <= qry, diag[i], -jnp.inf)
        m = jnp.max(s, axis=0, keepdims=True)
        p = jnp.exp2(s - m)
        m_scr[i] = m
        l_scr[i] = jnp.sum(p, axis=0, keepdims=True)
        ps.append(p.astype(BF16))
    for i, (g, c) in enumerate(chains):
        acc_scr[i] = jnp.dot(values(qi, g), ps[i], preferred_element_type=F32)

    def accumulate(kj, slot):
        ps = []
        for i in range(len(chains)):
            s = s_scr[slot, i]
            m_old = m_scr[i]
            m_new = jnp.maximum(m_old, jnp.max(s, axis=0, keepdims=True))
            alpha = jnp.exp2(m_old - m_new)
            p = jnp.exp2(s - m_new)
            l_scr[i] = alpha * l_scr[i] + jnp.sum(p, axis=0, keepdims=True)
            m_scr[i] = m_new
            ps.append((alpha, p.astype(BF16)))
        for i, (g, c) in enumerate(chains):
            alpha, p = ps[i]
            acc_scr[i] = alpha * acc_scr[i] + jnp.dot(values(kj, g), p, preferred_element_type=F32)

    def pair(i, carry):
        for j, s in enumerate(scores(2 * i + 1)):
            s_scr[1, j] = s
        accumulate(2 * i, 0)
        for j, s in enumerate(scores(jnp.minimum(2 * i + 2, qi))):
            s_scr[0, j] = s
        accumulate(2 * i + 1, 1)
        return carry

    lax.fori_loop(0, qi // 2, pair, 0)

    @pl.when(qi % 2 == 1)
    def _():
        accumulate(qi - 1, 0)

    if mode == "diff":
        p_ = lam_ref[...]
        lam = (jnp.exp(jnp.sum(p_[0:1] * p_[1:2], axis=-1, keepdims=True))
               - jnp.exp(jnp.sum(p_[2:3] * p_[3:4], axis=-1, keepdims=True)) + lam_init)
    for g in range(groups):
        o_a = acc_scr[2 * g] * (1.0 / l_scr[2 * g])
        o_b = acc_scr[2 * g + 1] * (1.0 / l_scr[2 * g + 1])
        cols = slice(g * LANES, (g + 1) * LANES)
        if mode == "diff":
            o = (o_a - lam * o_b).T
            o_ref[:, cols] = (_rmsnorm(o, subln_ref[...]) * (1.0 - lam_init)).astype(o_ref.dtype)
        else:
            top = lax.broadcasted_iota(jnp.int32, o_a.shape, 0) < (LANES // 2)
            o_ref[:, cols] = jnp.where(top, o_a, o_b).T.astype(o_ref.dtype)


def _attention(q, k, vt, *, mode, lam_params=None, subln=None, lam_init=0.0, out_dtype=BF16, groups_per_step=2):
    b, t, _ = q.shape
    blk = vt.shape[3]
    width = LANES if mode == "diff" else 2 * LANES
    total = vt.shape[2] // LANES
    groups = min(total, groups_per_step)
    assert t % blk == 0 and q.shape[2] == total * width and total % groups == 0
    in_specs = [pl.BlockSpec((None, blk, groups * width), lambda i, g, j: (i, j, g)),
                pl.BlockSpec((None, t, groups * width), lambda i, g, j: (i, 0, g)),
                pl.BlockSpec((None, t // blk, groups * LANES, blk), lambda i, g, j: (i, 0, g, 0))]
    args = [q, k, vt]
    if mode == "diff":
        in_specs += [_resident(lam_params.shape), _resident((1, LANES))]
        args += [lam_params, subln.reshape(1, LANES)]
    chains = 2 * groups
    return pl.pallas_call(
        functools.partial(_attn_kernel, blk=blk, mode=mode, groups=groups, lam_init=lam_init),
        grid=(b, total // groups, t // blk),
        in_specs=in_specs,
        out_specs=pl.BlockSpec((None, blk, groups * LANES), lambda i, g, j: (i, j, g)),
        out_shape=jax.ShapeDtypeStruct((b, t, total * LANES), out_dtype),
        scratch_shapes=[pltpu.VMEM((chains, 1, blk), F32), pltpu.VMEM((chains, 1, blk), F32),
                        pltpu.VMEM((chains, LANES, blk), F32), pltpu.VMEM((2, chains, blk, blk), F32)],
        compiler_params=_params("parallel", "parallel", "arbitrary"),
        name="attn_" + mode,
    )(*args)


def _out_kernel(h_ref, ya_ref, yb_ref, oc_ref, cn_ref, w_ref, o_ref, *, splits):
    yc = _rmsnorm(oc_ref[...], cn_ref[...]).astype(BF16)
    a_hi, b_hi = splits
    acc = jnp.dot(ya_ref[...], w_ref[:a_hi, :], preferred_element_type=F32)
    acc += jnp.dot(yb_ref[...], w_ref[a_hi:b_hi, :], preferred_element_type=F32)
    acc += jnp.dot(yc, w_ref[b_hi:, :], preferred_element_type=F32)
    o_ref[...] = h_ref[...] + acc


def _mixer_out(h, y_a, y_b, o_c, c_norm, w_out, *, row_tile=512):
    n, d = h.shape
    wa, wb, wc = y_a.shape[1], y_b.shape[1], o_c.shape[1]
    assert n % row_tile == 0 and wa + wb + wc == w_out.shape[0]
    rows = lambda width: pl.BlockSpec((row_tile, width), lambda i: (i, 0))
    return pl.pallas_call(
        functools.partial(_out_kernel, splits=(wa, wa + wb)),
        grid=(n // row_tile,),
        in_specs=[rows(d), rows(wa), rows(wb), rows(wc), _resident((1, wc)), _resident(w_out.shape)],
        out_specs=rows(d),
        out_shape=jax.ShapeDtypeStruct((n, d), F32),
        compiler_params=_params("parallel"),
        name="mixer_out",
    )(h, y_a, y_b, o_c, c_norm.reshape(1, wc), w_out)


def kernel(x, meta_tokens, ffn1_norm, ffn1_in, ffn1_out, mix_norm, w_in, conv_w, conv_b, lru_wa, lru_ba, lru_wx, lru_bx, lru_lambda, lru_out_norm, lam_q1, lam_k1, lam_q2, lam_k2, diff_subln, q_norm, w_uq, kv_norm, w_ukv, mla_out_norm, w_out, ffn2_norm, ffn2_in, ffn2_out, final_norm):
    b, s, d = x.shape
    depth = ffn1_in.shape[0]
    t_real = N_META + s
    t = -(-t_real // SEQ_BLOCK) * SEQ_BLOCK
    meta = jnp.broadcast_to(meta_tokens[None].astype(x.dtype), (b, N_META, d))
    h = jnp.concatenate([meta, x, jnp.zeros((b, t - t_real, d), x.dtype)], axis=1).reshape(b * t, d)
    n = b * t
    tables = _rope_tables(t)
    lru_chunk = t // 4 if (t // 4) % (2 * SUBLANES) == 0 else SEQ_BLOCK

    for l in range(depth):
        h = _ffn(h, ffn1_norm[l], ffn1_in[l].astype(BF16), ffn1_out[l].astype(BF16))
        g_a, u_a, q_d, k_d, vt_d, q_c, k_c, vt_c = _mixer_proj(
            h.reshape(b, t, d), mix_norm[l], w_in[l], q_norm[l], w_uq[l], kv_norm[l], w_ukv[l], tables)
        y_a = _lru(g_a, u_a, conv_w[l], conv_b[l], lru_wa[l], lru_ba[l], lru_wx[l], lru_bx[l], lru_lambda[l],
                   lru_out_norm[l], chunk=lru_chunk)
        lam_init = 0.8 - 0.6 * math.exp(-0.3 * l)
        lam_params = jnp.stack([lam_q1[l], lam_k1[l], lam_q2[l], lam_k2[l]]).astype(F32)
        y_b = _attention(q_d, k_d, vt_d, mode="diff", lam_params=lam_params, subln=diff_subln[l],
                         lam_init=lam_init)
        o_c = _attention(q_c, k_c, vt_c, mode="pair", out_dtype=F32)
        h = _mixer_out(h, y_a.reshape(n, -1), y_b.reshape(n, -1), o_c.reshape(n, -1), mla_out_norm[l],
                       w_out[l].astype(BF16))
        h = _ffn(h, ffn2_norm[l], ffn2_in[l].astype(BF16), ffn2_out[l].astype(BF16),
                 final_gain=final_norm if l == depth - 1 else None)
    return h.reshape(b, t, d)[:, N_META:N_META + s]
```

```python
import functools
import math

import jax
import jax.numpy as jnp
from jax import lax
from jax.experimental import pallas as pl
from jax.experimental.pallas import tpu as pltpu

F32 = jnp.float32
BF16 = jnp.bfloat16

N_META = 16
NORM_EPS = 1e-6
CONV_WIDTH = 4
LRU_C = 8.0
DIFF_QK_DIM = 64
MLA_HEADS = 4
MLA_NOPE_DIM = 64
MLA_ROPE_DIM = 32
MLA_V_DIM = 64
ROPE_THETA = 10000.0
LOG2_E = math.log2(math.e)

LANES = 128
SUBLANES = 8
MXU_DIM = 256
SEQ_BLOCK = MXU_DIM
VMEM_LIMIT_BYTES = 56 * 1024 * 1024

_NT = (((1,), (1,)), ((), ()))


def _params(*semantics):
    return pltpu.CompilerParams(dimension_semantics=semantics, vmem_limit_bytes=VMEM_LIMIT_BYTES)


def _resident(shape):
    zeros = (0,) * len(shape)
    return pl.BlockSpec(shape, lambda *_: zeros, pipeline_mode=pl.Buffered(1))


def _rmsnorm(x, gain):
    return x * lax.rsqrt(jnp.mean(x * x, axis=-1, keepdims=True) + NORM_EPS) * gain


def _mixer_out_rows(ya_ref, yb_ref, oc_ref, cn_ref, w_ref):
    a_hi = ya_ref.shape[1]
    b_hi = a_hi + yb_ref.shape[1]
    yc = _rmsnorm(oc_ref[...], cn_ref[...]).astype(BF16)
    acc = jnp.dot(ya_ref[...], w_ref[:a_hi, :], preferred_element_type=F32)
    acc += jnp.dot(yb_ref[...], w_ref[a_hi:b_hi, :], preferred_element_type=F32)
    return acc + jnp.dot(yc, w_ref[b_hi:, :], preferred_element_type=F32)


def _ffn_kernel(h_ref, gain_ref, w_in_ref, w_out_ref, *rest, d_ff, ff_chunk, final_norm, with_mixer):
    if with_mixer:
        mix_refs, rest = rest[:5], rest[5:]
    if final_norm:
        final_gain_ref, o_ref, acc_ref = rest
    else:
        o_ref, acc_ref = rest
    x = h_ref[...]
    if with_mixer:
        x = x + _mixer_out_rows(*mix_refs)
    xn = _rmsnorm(x, gain_ref[...]).astype(BF16)
    for j in range(d_ff // ff_chunk):
        lo = j * ff_chunk
        gate = jnp.dot(xn, w_in_ref[:, lo:lo + ff_chunk], preferred_element_type=F32)
        up = jnp.dot(xn, w_in_ref[:, d_ff + lo:d_ff + lo + ff_chunk], preferred_element_type=F32)
        act = (jax.nn.silu(gate) * up).astype(BF16)
        part = jnp.dot(act, w_out_ref[lo:lo + ff_chunk, :], preferred_element_type=F32)
        if j == 0:
            acc_ref[...] = part
        else:
            acc_ref[...] += part
    y = x + 0.5 * acc_ref[...]
    if final_norm:
        y = _rmsnorm(y, final_gain_ref[...])
    o_ref[...] = y


def _ffn(h, gain, w_in, w_out, final_gain=None, mixer=None, *, row_tile=512, ff_chunk=256):
    n, d = h.shape
    d_ff = w_out.shape[0]
    assert n % row_tile == 0 and d_ff % ff_chunk == 0
    final_norm = final_gain is not None
    rows = lambda width: pl.BlockSpec((row_tile, width), lambda i: (i, 0))
    in_specs = [rows(d), _resident((1, d)), _resident(w_in.shape), _resident(w_out.shape)]
    args = [h, gain.reshape(1, d), w_in, w_out]
    if mixer is not None:
        y_a, y_b, o_c, c_norm, w_mix = mixer
        assert y_a.shape[1] + y_b.shape[1] + o_c.shape[1] == w_mix.shape[0]
        in_specs += [rows(y_a.shape[1]), rows(y_b.shape[1]), rows(o_c.shape[1]), _resident((1, o_c.shape[1])),
                     _resident(w_mix.shape)]
        args += [y_a, y_b, o_c, c_norm.reshape(1, -1), w_mix]
    if final_norm:
        in_specs.append(_resident((1, d)))
        args.append(final_gain.reshape(1, d))
    return pl.pallas_call(
        functools.partial(_ffn_kernel, d_ff=d_ff, ff_chunk=ff_chunk, final_norm=final_norm,
                          with_mixer=mixer is not None),
        grid=(n // row_tile,),
        in_specs=in_specs,
        out_specs=pl.BlockSpec((row_tile, d), lambda i: (i, 0)),
        out_shape=jax.ShapeDtypeStruct((n, d), F32),
        scratch_shapes=[pltpu.VMEM((row_tile, d), F32)],
        compiler_params=_params("parallel"),
        name="ffn",
    )(*args)


def _rope(x, cos, sin_lo, sin_hi):
    half = MLA_ROPE_DIM // 2
    return x * cos + pltpu.roll(x, LANES - half, 1) * sin_lo + pltpu.roll(x, half, 1) * sin_hi


def _proj_kernel(h_ref, gain_ref, w_ref, wvt_ref, qn_ref, wuq_ref, kvn_ref, wuk_ref, wuvt_ref,
                 cos_ref, slo_ref, shi_ref,
                 g_ref, u_ref, qd_ref, kd_ref, vdt_ref, qc_ref, kc_ref, vct_ref, *, offs):
    xn = _rmsnorm(h_ref[...], gain_ref[...]).astype(BF16)

    def proj(name):
        lo, hi = offs[name]
        return jnp.dot(xn, w_ref[:, lo:hi], preferred_element_type=F32)

    g_ref[...] = proj("g_a")
    u_ref[...] = proj("u_a")
    qd_ref[...] = (proj("q_d") * (DIFF_QK_DIM ** -0.5 * LOG2_E)).astype(BF16)
    kd_ref[...] = proj("k_d").astype(BF16)
    vdt_ref[...] = lax.dot_general(wvt_ref[...], xn, _NT, preferred_element_type=F32).astype(BF16)

    cos, slo, shi = cos_ref[...], slo_ref[...], shi_ref[...]
    cq = _rmsnorm(proj("c_q"), qn_ref[...]).astype(BF16)
    q = jnp.dot(cq, wuq_ref[...], preferred_element_type=F32)
    ckv = _rmsnorm(proj("c_kv"), kvn_ref[...]).astype(BF16)
    k_nope = jnp.dot(ckv, wuk_ref[...], preferred_element_type=F32)
    vct_ref[...] = lax.dot_general(wuvt_ref[...], ckv, _NT, preferred_element_type=F32).astype(BF16)
    k_rope = _rope(proj("k_r"), cos, slo, shi)
    q_scale = (MLA_NOPE_DIM + MLA_ROPE_DIM) ** -0.5 * LOG2_E
    for hd in range(MLA_HEADS):
        sl = slice(hd * LANES, (hd + 1) * LANES)
        qc_ref[:, sl] = (_rope(q[:, sl], cos, slo, shi) * q_scale).astype(BF16)
        kc_ref[:, sl] = (k_nope[:, sl] + k_rope).astype(BF16)


def _rope_tables(first_pos, t_len):
    half = MLA_ROPE_DIM // 2
    inv_freq = ROPE_THETA ** (-jnp.arange(half, dtype=F32) / half)
    pos = jnp.arange(first_pos, first_pos + t_len, dtype=jnp.int32)
    ang = pos.astype(F32)[:, None] * inv_freq[None, :]
    cos, sin = jnp.cos(ang), jnp.sin(ang)
    zeros = jnp.zeros((t_len, half), F32)
    ones = jnp.ones((t_len, MLA_NOPE_DIM), F32)
    tail = jnp.zeros((t_len, LANES - MLA_NOPE_DIM - MLA_ROPE_DIM), F32)
    cos_t = jnp.concatenate([ones, cos, cos, tail], axis=1)
    sin_lo = jnp.concatenate([0 * ones, -sin, zeros, tail], axis=1)
    sin_hi = jnp.concatenate([0 * ones, zeros, sin, tail], axis=1)
    return cos_t, sin_lo, sin_hi


def _pad_heads(w, heads, width, lo, hi):
    k = w.shape[0]
    w = w.reshape(k, heads, width)[:, :, lo:hi]
    w = jnp.pad(w, ((0, 0), (0, 0), (0, LANES - (hi - lo))))
    return w.reshape(k, heads * LANES)


def _mixer_proj(h3, gain, w_in, q_norm, w_uq, kv_norm, w_ukv, tables, *, blk):
    b, t, d = h3.shape
    assert t % blk == 0 and blk % LANES == 0
    widths = dict(g_a=256, u_a=256, q_d=512, k_d=512, v_d=512, c_q=w_uq.shape[0], c_kv=w_ukv.shape[0],
                  k_r=MLA_ROPE_DIM)
    offs, lo = {}, 0
    for name, wd in widths.items():
        offs[name] = (lo, lo + wd)
        lo += wd
    assert lo == w_in.shape[1]
    kr_lo, kr_hi = offs["k_r"]
    w_kr = jnp.pad(w_in[:, kr_lo:kr_hi], ((0, 0), (MLA_NOPE_DIM, LANES - MLA_NOPE_DIM - MLA_ROPE_DIM)))
    w_all = jnp.concatenate([w_in[:, :kr_lo], w_kr], axis=1).astype(BF16)
    offs["k_r"] = (kr_lo, kr_lo + LANES)
    v_lo, v_hi = offs["v_d"]
    w_vt = w_in[:, v_lo:v_hi].T.astype(BF16)
    qk = MLA_NOPE_DIM + MLA_ROPE_DIM
    w_uq_p = _pad_heads(w_uq, MLA_HEADS, qk, 0, qk).astype(BF16)
    w_uk_p = _pad_heads(w_ukv, MLA_HEADS, MLA_NOPE_DIM + MLA_V_DIM, 0, MLA_NOPE_DIM).astype(BF16)
    w_uv = w_ukv.reshape(-1, MLA_HEADS, MLA_NOPE_DIM + MLA_V_DIM)[:, :, MLA_NOPE_DIM:]
    w_uvt = w_uv.reshape(-1, MLA_HEADS * MLA_V_DIM).T.astype(BF16)

    def rows(width, dtype):
        return (pl.BlockSpec((None, blk, width), lambda i, j: (i, j, 0)),
                jax.ShapeDtypeStruct((b, t, width), dtype))

    def cols(width):
        return (pl.BlockSpec((None, None, width, blk), lambda i, j: (i, j, 0, 0)),
                jax.ShapeDtypeStruct((b, t // blk, width, blk), BF16))

    outs = [rows(256, F32), rows(256, F32), rows(512, BF16), rows(512, BF16), cols(512),
            rows(MLA_HEADS * LANES, BF16), rows(MLA_HEADS * LANES, BF16), cols(MLA_HEADS * MLA_V_DIM)]
    table_spec = pl.BlockSpec((blk, LANES), lambda i, j: (j, 0))
    return pl.pallas_call(
        functools.partial(_proj_kernel, offs=offs),
        grid=(b, t // blk),
        in_specs=[pl.BlockSpec((None, blk, d), lambda i, j: (i, j, 0)), _resident((1, d)),
                  _resident(w_all.shape), _resident(w_vt.shape), _resident((1, w_uq.shape[0])),
                  _resident(w_uq_p.shape), _resident((1, w_ukv.shape[0])), _resident(w_uk_p.shape),
                  _resident(w_uvt.shape), table_spec, table_spec, table_spec],
        out_specs=[o[0] for o in outs],
        out_shape=[o[1] for o in outs],
        compiler_params=_params("parallel", "parallel"),
        name="mixer_proj",
    )(h3, gain.reshape(1, d), w_all, w_vt, q_norm.reshape(1, -1), w_uq_p, kv_norm.reshape(1, -1), w_uk_p, w_uvt,
      *tables)


def _lru_kernel(g_ref, u_ref, cw_ref, cb_ref, wa_ref, ba_ref, wx_ref, bx_ref, lam_ref, on_ref, halo0_ref, carry0_ref,
                y_ref, halo_out_ref, carry_out_ref, halo_ref, carry_ref, a_scr, b_scr, h_scr, *, chunk):
    @pl.when(pl.program_id(1) == 0)
    def _():
        halo_ref[...] = halo0_ref[...]
        carry_ref[...] = carry0_ref[...]

    u = u_ref[...]
    ext = jnp.concatenate([halo_ref[...], u], axis=0)
    halo_ref[...] = u[chunk - SUBLANES:, :]
    conv = cb_ref[...] + cw_ref[CONV_WIDTH - 1:CONV_WIDTH, :] * u
    for j in range(CONV_WIDTH - 1):
        shifted = pltpu.roll(ext, CONV_WIDTH - 1 - j, 0)[SUBLANES:, :]
        conv = conv + cw_ref[j:j + 1, :] * shifted

    ub = conv.astype(BF16)
    r = jax.nn.sigmoid(jnp.dot(ub, wa_ref[...], preferred_element_type=F32) + ba_ref[...])
    i = jax.nn.sigmoid(jnp.dot(ub, wx_ref[...], preferred_element_type=F32) + bx_ref[...])
    neg_lam = -lam_ref[...]
    softplus = jnp.maximum(neg_lam, 0.0) + jnp.log1p(jnp.exp(-jnp.abs(neg_lam)))
    log_a = -LRU_C * r * softplus
    a = jnp.exp(log_a)
    b = jnp.sqrt(-jnp.tanh(log_a) * (a * a + 1.0)) * (i * conv)

    row = jnp.bitwise_and(lax.broadcasted_iota(jnp.int32, a.shape, 0), SUBLANES - 1)
    for s in (1, 2, 4):
        keep = row >= s
        b = jnp.where(keep, a * pltpu.roll(b, s, 0) + b, b)
        a = jnp.where(keep, a * pltpu.roll(a, s, 0), a)
    a_scr[...] = a
    b_scr[...] = b

    def group(gi, carry):
        sl = pl.ds(pl.multiple_of(gi * SUBLANES, SUBLANES), SUBLANES)
        h = a_scr[sl, :] * carry + b_scr[sl, :]
        h_scr[sl, :] = h
        return jnp.broadcast_to(h[SUBLANES - 1:SUBLANES, :], h.shape)

    carry = lax.fori_loop(0, chunk // SUBLANES, group, carry_ref[...])
    carry_ref[...] = carry
    carry_out_ref[...] = carry
    halo_out_ref[...] = u[chunk - SUBLANES:, :]
    y = h_scr[...] * jax.nn.gelu(g_ref[...])
    y_ref[...] = _rmsnorm(y, on_ref[...]).astype(BF16)


def _block_diag(w):
    n, c, d = w.shape
    eye = jnp.eye(n, dtype=w.dtype)
    return (w[:, :, None, :] * eye[:, None, :, None]).reshape(n * c, n * d)


def _lru(g, u, conv_w, conv_b, wa, ba, wx, bx, lam, out_norm, *, chunk, halo0=None, carry0=None):
    b, t, c = u.shape
    assert t % chunk == 0 and chunk % (2 * SUBLANES) == 0
    if halo0 is None:
        halo0 = carry0 = jnp.zeros((SUBLANES, c), F32)
    row = lambda v: v.reshape(1, c)
    seq = pl.BlockSpec((None, chunk, c), lambda i, j: (i, j, 0))
    ctx = pl.BlockSpec((None, None, SUBLANES, c), lambda i, j: (i, j, 0, 0))
    ctx_shape = jax.ShapeDtypeStruct((b, t // chunk, SUBLANES, c), F32)
    return pl.pallas_call(
        functools.partial(_lru_kernel, chunk=chunk),
        grid=(b, t // chunk),
        in_specs=[seq, seq, _resident((CONV_WIDTH, c)), _resident((1, c)), _resident((c, c)), _resident((1, c)),
                  _resident((c, c)), _resident((1, c)), _resident((1, c)), _resident((1, c)),
                  _resident((SUBLANES, c)), _resident((SUBLANES, c))],
        out_specs=[seq, ctx, ctx],
        out_shape=[jax.ShapeDtypeStruct((b, t, c), BF16), ctx_shape, ctx_shape],
        scratch_shapes=[pltpu.VMEM((SUBLANES, c), F32), pltpu.VMEM((SUBLANES, c), F32),
                        pltpu.VMEM((chunk, c), F32), pltpu.VMEM((chunk, c), F32), pltpu.VMEM((chunk, c), F32)],
        compiler_params=_params("parallel", "arbitrary"),
        name="rg_lru",
    )(g, u, conv_w, row(conv_b), _block_diag(wa).astype(BF16), row(ba), _block_diag(wx).astype(BF16), row(bx),
      row(lam), row(out_norm), halo0, carry0)


def _attn_kernel(q_ref, k_ref, vt_ref, *rest, blk, mode, groups, lam_init, has_prefix):
    if has_prefix:
        kpre_ref, vtpre_ref, *rest = rest
    if mode == "diff":
        lam_ref, subln_ref, o_ref, m_scr, l_scr, acc_scr, s_scr = rest
    else:
        o_ref, m_scr, l_scr, acc_scr, s_scr = rest
    qi = pl.program_id(2)
    width = LANES if mode == "diff" else 2 * LANES
    v_rows = LANES if mode == "diff" else LANES // 2
    chains = [(g, c) for g in range(groups) for c in range(2)]

    def k_lanes(i):
        g, c = chains[i]
        lo = g * width + (0 if mode == "diff" else c * LANES)
        return slice(lo, lo + LANES)

    def v_sublanes(i):
        lo = (i // 2) * LANES if mode == "diff" else i * v_rows
        return slice(lo, lo + v_rows)

    qs = []
    for g in range(groups):
        if mode == "diff":
            q = q_ref[:, g * width:(g + 1) * width]
            low = lax.broadcasted_iota(jnp.int32, q.shape, 1) < (LANES // 2)
            zero = jnp.zeros_like(q)
            qs += [jnp.where(low, q, zero), jnp.where(low, zero, q)]
        else:
            qs += [q_ref[:, g * width:g * width + LANES], q_ref[:, g * width + LANES:(g + 1) * width]]

    def scores(kj):
        rows = pl.ds(pl.multiple_of(kj * blk, blk), blk)
        return [lax.dot_general(k_ref[rows, k_lanes(i)], qs[i], _NT, preferred_element_type=F32)
                for i in range(len(chains))]

    def values(kj, i):
        return vt_ref[kj, v_sublanes(i), :]

    diag = scores(qi)
    if has_prefix:
        pre = [lax.dot_general(kpre_ref[:, k_lanes(i)], qs[i], _NT, preferred_element_type=F32)
               for i in range(len(chains))]
    for i, s in enumerate(scores(0)):
        s_scr[0, i] = s
    ps = []
    for i in range(len(chains)):
        key = lax.broadcasted_iota(jnp.int32, diag[i].shape, 0)
        qry = lax.broadcasted_iota(jnp.int32, diag[i].shape, 1)
        s = jnp.where(key <= qry, diag[i], -jnp.inf)
        m = jnp.max(s, axis=0, keepdims=True)
        if has_prefix:
            m = jnp.maximum(m, jnp.max(pre[i], axis=0, keepdims=True))
        p = jnp.exp2(s - m)
        l = jnp.sum(p, axis=0, keepdims=True)
        if has_prefix:
            p_pre = jnp.exp2(pre[i] - m)
            l = l + jnp.sum(p_pre, axis=0, keepdims=True)
            ps.append((p.astype(BF16), p_pre.astype(BF16)))
        else:
            ps.append((p.astype(BF16), None))
        m_scr[i] = m
        l_scr[i] = l
    for i in range(len(chains)):
        p, p_pre = ps[i]
        acc = jnp.dot(values(qi, i), p, preferred_element_type=F32)
        if has_prefix:
            acc = acc + jnp.dot(vtpre_ref[v_sublanes(i), :], p_pre, preferred_element_type=F32)
        acc_scr[i] = acc

    def accumulate(kj, slot):
        ps = []
        for i in range(len(chains)):
            s = s_scr[slot, i]
            m_old = m_scr[i]
            m_new = jnp.maximum(m_old, jnp.max(s, axis=0, keepdims=True))
            alpha = jnp.exp2(m_old - m_new)
            p = jnp.exp2(s - m_new)
            l_scr[i] = alpha * l_scr[i] + jnp.sum(p, axis=0, keepdims=True)
            m_scr[i] = m_new
            ps.append((alpha, p.astype(BF16)))
        for i in range(len(chains)):
            alpha, p = ps[i]
            acc_scr[i] = alpha * acc_scr[i] + jnp.dot(values(kj, i), p, preferred_element_type=F32)

    def pair(i, carry):
        for j, s in enumerate(scores(2 * i + 1)):
            s_scr[1, j] = s
        accumulate(2 * i, 0)
        for j, s in enumerate(scores(jnp.minimum(2 * i + 2, qi))):
            s_scr[0, j] = s
        accumulate(2 * i + 1, 1)
        return carry

    lax.fori_loop(0, qi // 2, pair, 0)

    @pl.when(qi % 2 == 1)
    def _():
        accumulate(qi - 1, 0)

    if mode == "diff":
        p_ = lam_ref[...]
        lam = (jnp.exp(jnp.sum(p_[0:1] * p_[1:2], axis=-1, keepdims=True))
               - jnp.exp(jnp.sum(p_[2:3] * p_[3:4], axis=-1, keepdims=True)) + lam_init)
    for g in range(groups):
        o_a = acc_scr[2 * g] * (1.0 / l_scr[2 * g])
        o_b = acc_scr[2 * g + 1] * (1.0 / l_scr[2 * g + 1])
        cols = slice(g * LANES, (g + 1) * LANES)
        if mode == "diff":
            o = (o_a - lam * o_b).T
            o_ref[:, cols] = (_rmsnorm(o, subln_ref[...]) * (1.0 - lam_init)).astype(o_ref.dtype)
        else:
            o_ref[:, cols] = jnp.concatenate([o_a, o_b], axis=0).T.astype(o_ref.dtype)


def _attention(q, k, vt, *, mode, prefix=None, lam_params=None, subln=None, lam_init=0.0, out_dtype=BF16,
               groups_per_step=2):
    b, t, _ = q.shape
    blk = vt.shape[3]
    width = LANES if mode == "diff" else 2 * LANES
    total = vt.shape[2] // LANES
    groups = min(total, groups_per_step)
    assert t % blk == 0 and q.shape[2] == total * width and total % groups == 0
    in_specs = [pl.BlockSpec((None, blk, groups * width), lambda i, g, j: (i, j, g)),
                pl.BlockSpec((None, t, groups * width), lambda i, g, j: (i, 0, g)),
                pl.BlockSpec((None, t // blk, groups * LANES, blk), lambda i, g, j: (i, 0, g, 0))]
    args = [q, k, vt]
    if prefix is not None:
        k_pre, vt_pre = prefix
        n_pre = k_pre.shape[0]
        in_specs += [pl.BlockSpec((n_pre, groups * width), lambda i, g, j: (0, g)),
                     pl.BlockSpec((groups * LANES, n_pre), lambda i, g, j: (g, 0))]
        args += [k_pre, vt_pre]
    if mode == "diff":
        in_specs += [_resident(lam_params.shape), _resident((1, LANES))]
        args += [lam_params, subln.reshape(1, LANES)]
    chains = 2 * groups
    v_rows = LANES if mode == "diff" else LANES // 2
    return pl.pallas_call(
        functools.partial(_attn_kernel, blk=blk, mode=mode, groups=groups, lam_init=lam_init,
                          has_prefix=prefix is not None),
        grid=(b, total // groups, t // blk),
        in_specs=in_specs,
        out_specs=pl.BlockSpec((None, blk, groups * LANES), lambda i, g, j: (i, j, g)),
        out_shape=jax.ShapeDtypeStruct((b, t, total * LANES), out_dtype),
        scratch_shapes=[pltpu.VMEM((chains, 1, blk), F32), pltpu.VMEM((chains, 1, blk), F32),
                        pltpu.VMEM((chains, v_rows, blk), F32), pltpu.VMEM((2, chains, blk, blk), F32)],
        compiler_params=_params("parallel", "parallel", "arbitrary"),
        name="attn_" + mode,
    )(*args)


def kernel(x, meta_tokens, ffn1_norm, ffn1_in, ffn1_out, mix_norm, w_in, conv_w, conv_b, lru_wa, lru_ba, lru_wx, lru_bx, lru_lambda, lru_out_norm, lam_q1, lam_k1, lam_q2, lam_k2, diff_subln, q_norm, w_uq, kv_norm, w_ukv, mla_out_norm, w_out, ffn2_norm, ffn2_in, ffn2_out, final_norm):
    b, s, d = x.shape
    depth = ffn1_in.shape[0]
    t = -(-s // SEQ_BLOCK) * SEQ_BLOCK
    t_meta = LANES
    assert N_META % (2 * SUBLANES) == 0 and N_META <= t_meta
    h = jnp.pad(x, ((0, 0), (0, t - s), (0, 0))).reshape(b * t, d)
    h_meta = jnp.pad(meta_tokens.astype(x.dtype), ((0, t_meta - N_META), (0, 0)))
    tables = _rope_tables(N_META, t)
    tables_meta = _rope_tables(0, t_meta)
    lru_chunk = t // 4 if (t // 4) % (2 * SUBLANES) == 0 else SEQ_BLOCK

    def mixer(hs, l, *, seq, blk, chunk, rope, prefix):
        g_a, u_a, q_d, k_d, vt_d, q_c, k_c, vt_c = _mixer_proj(
            hs.reshape(-1, seq, d), mix_norm[l], w_in[l], q_norm[l], w_uq[l], kv_norm[l], w_ukv[l], rope, blk=blk)
        halo0, carry0, pre_d, pre_c = prefix if prefix is not None else (None,) * 4
        y_a, halo, carry = _lru(g_a, u_a, conv_w[l], conv_b[l], lru_wa[l], lru_ba[l], lru_wx[l], lru_bx[l],
                                lru_lambda[l], lru_out_norm[l], chunk=chunk, halo0=halo0, carry0=carry0)
        lam_init = 0.8 - 0.6 * math.exp(-0.3 * l)
        lam_params = jnp.stack([lam_q1[l], lam_k1[l], lam_q2[l], lam_k2[l]]).astype(F32)
        y_b = _attention(q_d, k_d, vt_d, mode="diff", prefix=pre_d, lam_params=lam_params, subln=diff_subln[l],
                         lam_init=lam_init)
        o_c = _attention(q_c, k_c, vt_c, mode="pair", prefix=pre_c, out_dtype=F32)
        export = (halo[0, 0], carry[0, 0], (k_d[0, :N_META], vt_d[0, 0, :, :N_META]),
                  (k_c[0, :N_META], vt_c[0, 0, :, :N_META]))
        n = hs.shape[0]
        return (y_a.reshape(n, -1), y_b.reshape(n, -1), o_c.reshape(n, -1)), export

    for l in range(depth):
        w1 = (ffn1_norm[l], ffn1_in[l].astype(BF16), ffn1_out[l].astype(BF16))
        w2 = (ffn2_norm[l], ffn2_in[l].astype(BF16), ffn2_out[l].astype(BF16))
        w_mix_out = w_out[l].astype(BF16)
        last = l == depth - 1
        h_meta = _ffn(h_meta, *w1, row_tile=t_meta)
        ys_meta, prefix = mixer(h_meta, l, seq=t_meta, blk=t_meta, chunk=N_META, rope=tables_meta, prefix=None)
        h = _ffn(h, *w1)
        ys, _ = mixer(h, l, seq=t, blk=SEQ_BLOCK, chunk=lru_chunk, rope=tables, prefix=prefix)
        h = _ffn(h, *w2, final_gain=final_norm if last else None, mixer=(*ys, mla_out_norm[l], w_mix_out))
        if not last:
            h_meta = _ffn(h_meta, *w2, mixer=(*ys_meta, mla_out_norm[l], w_mix_out), row_tile=t_meta)
    return h.reshape(b, t, d)[:, :s]
```

```python
import functools
import math

import jax
import jax.numpy as jnp
from jax import lax
from jax.experimental import pallas as pl
from jax.experimental.pallas import tpu as pltpu

F32 = jnp.float32
BF16 = jnp.bfloat16

N_META = 16
NORM_EPS = 1e-6
CONV_WIDTH = 4
LRU_C = 8.0
DIFF_QK_DIM = 64
MLA_HEADS = 4
MLA_NOPE_DIM = 64
MLA_ROPE_DIM = 32
MLA_V_DIM = 64
ROPE_THETA = 10000.0
LOG2_E = math.log2(math.e)

LANES = 128
SUBLANES = 8
MXU_DIM = 256
SEQ_BLOCK = MXU_DIM
VMEM_LIMIT_BYTES = 56 * 1024 * 1024

_NT = (((1,), (1,)), ((), ()))


def _params(*semantics):
    return pltpu.CompilerParams(dimension_semantics=semantics, vmem_limit_bytes=VMEM_LIMIT_BYTES)


def _resident(shape):
    zeros = (0,) * len(shape)
    return pl.BlockSpec(shape, lambda *_: zeros, pipeline_mode=pl.Buffered(1))


def _rmsnorm(x, gain):
    return x * lax.rsqrt(jnp.mean(x * x, axis=-1, keepdims=True) + NORM_EPS) * gain


def _mixer_out_rows(ya_ref, yb_ref, oc_ref, cn_ref, w_ref):
    a_hi = ya_ref.shape[1]
    b_hi = a_hi + yb_ref.shape[1]
    yc = _rmsnorm(oc_ref[...], cn_ref[...]).astype(BF16)
    acc = jnp.dot(ya_ref[...], w_ref[:a_hi, :], preferred_element_type=F32)
    acc += jnp.dot(yb_ref[...], w_ref[a_hi:b_hi, :], preferred_element_type=F32)
    return acc + jnp.dot(yc, w_ref[b_hi:, :], preferred_element_type=F32)


def _ffn_kernel(h_ref, gain_ref, w_in_ref, w_out_ref, *rest, d_ff, ff_chunk, final_norm, with_mixer):
    if with_mixer:
        mix_refs, rest = rest[:5], rest[5:]
    if final_norm:
        final_gain_ref, o_ref, acc_ref = rest
    else:
        o_ref, acc_ref = rest
    x = h_ref[...]
    if with_mixer:
        x = x + _mixer_out_rows(*mix_refs)
    xn = _rmsnorm(x, gain_ref[...]).astype(BF16)
    for j in range(d_ff // ff_chunk):
        lo = j * ff_chunk
        gate = jnp.dot(xn, w_in_ref[:, lo:lo + ff_chunk], preferred_element_type=F32)
        up = jnp.dot(xn, w_in_ref[:, d_ff + lo:d_ff + lo + ff_chunk], preferred_element_type=F32)
        act = (jax.nn.silu(gate) * up).astype(BF16)
        part = jnp.dot(act, w_out_ref[lo:lo + ff_chunk, :], preferred_element_type=F32)
        if j == 0:
            acc_ref[...] = part
        else:
            acc_ref[...] += part
    y = x + 0.5 * acc_ref[...]
    if final_norm:
        y = _rmsnorm(y, final_gain_ref[...])
    o_ref[...] = y


def _ffn(h, gain, w_in, w_out, final_gain=None, mixer=None, *, row_tile=512, ff_chunk=256):
    n, d = h.shape
    d_ff = w_out.shape[0]
    assert n % row_tile == 0 and d_ff % ff_chunk == 0
    final_norm = final_gain is not None
    rows = lambda width: pl.BlockSpec((row_tile, width), lambda i: (i, 0))
    in_specs = [rows(d), _resident((1, d)), _resident(w_in.shape), _resident(w_out.shape)]
    args = [h, gain.reshape(1, d), w_in, w_out]
    if mixer is not None:
        y_a, y_b, o_c, c_norm, w_mix = mixer
        assert y_a.shape[1] + y_b.shape[1] + o_c.shape[1] == w_mix.shape[0]
        in_specs += [rows(y_a.shape[1]), rows(y_b.shape[1]), rows(o_c.shape[1]), _resident((1, o_c.shape[1])),
                     _resident(w_mix.shape)]
        args += [y_a, y_b, o_c, c_norm.reshape(1, -1), w_mix]
    if final_norm:
        in_specs.append(_resident((1, d)))
        args.append(final_gain.reshape(1, d))
    return pl.pallas_call(
        functools.partial(_ffn_kernel, d_ff=d_ff, ff_chunk=ff_chunk, final_norm=final_norm,
                          with_mixer=mixer is not None),
        grid=(n // row_tile,),
        in_specs=in_specs,
        out_specs=pl.BlockSpec((row_tile, d), lambda i: (i, 0)),
        out_shape=jax.ShapeDtypeStruct((n, d), F32),
        scratch_shapes=[pltpu.VMEM((row_tile, d), F32)],
        compiler_params=_params("parallel"),
        name="ffn",
    )(*args)


def _rope(x, cos, sin_lo, sin_hi):
    half = MLA_ROPE_DIM // 2
    return x * cos + pltpu.roll(x, LANES - half, 1) * sin_lo + pltpu.roll(x, half, 1) * sin_hi


def _proj_kernel(h_ref, gain_ref, w_ref, wvt_ref, qn_ref, wuq_ref, kvn_ref, wuk_ref, wuvt_ref,
                 cos_ref, slo_ref, shi_ref,
                 g_ref, u_ref, qd_ref, kd_ref, vdt_ref, qc_ref, kc_ref, vct_ref, *, offs):
    xn = _rmsnorm(h_ref[...], gain_ref[...]).astype(BF16)

    def proj(name):
        lo, hi = offs[name]
        return jnp.dot(xn, w_ref[:, lo:hi], preferred_element_type=F32)

    g_ref[...] = proj("g_a")
    u_ref[...] = proj("u_a")
    qd_ref[...] = (proj("q_d") * (DIFF_QK_DIM ** -0.5 * LOG2_E)).astype(BF16)
    kd_ref[...] = proj("k_d").astype(BF16)
    vdt_ref[...] = lax.dot_general(wvt_ref[...], xn, _NT, preferred_element_type=F32).astype(BF16)

    cos, slo, shi = cos_ref[...], slo_ref[...], shi_ref[...]
    cq = _rmsnorm(proj("c_q"), qn_ref[...]).astype(BF16)
    q = jnp.dot(cq, wuq_ref[...], preferred_element_type=F32)
    ckv = _rmsnorm(proj("c_kv"), kvn_ref[...]).astype(BF16)
    k_nope = jnp.dot(ckv, wuk_ref[...], preferred_element_type=F32)
    vct_ref[...] = lax.dot_general(wuvt_ref[...], ckv, _NT, preferred_element_type=F32).astype(BF16)
    k_rope = _rope(proj("k_r"), cos, slo, shi)
    q_scale = (MLA_NOPE_DIM + MLA_ROPE_DIM) ** -0.5 * LOG2_E
    for hd in range(MLA_HEADS):
        sl = slice(hd * LANES, (hd + 1) * LANES)
        qc_ref[:, sl] = (_rope(q[:, sl], cos, slo, shi) * q_scale).astype(BF16)
        kc_ref[:, sl] = (k_nope[:, sl] + k_rope).astype(BF16)


def _rope_tables(first_pos, t_len):
    half = MLA_ROPE_DIM // 2
    inv_freq = ROPE_THETA ** (-jnp.arange(half, dtype=F32) / half)
    pos = jnp.arange(first_pos, first_pos + t_len, dtype=jnp.int32)
    ang = pos.astype(F32)[:, None] * inv_freq[None, :]
    cos, sin = jnp.cos(ang), jnp.sin(ang)
    zeros = jnp.zeros((t_len, half), F32)
    ones = jnp.ones((t_len, MLA_NOPE_DIM), F32)
    tail = jnp.zeros((t_len, LANES - MLA_NOPE_DIM - MLA_ROPE_DIM), F32)
    cos_t = jnp.concatenate([ones, cos, cos, tail], axis=1)
    sin_lo = jnp.concatenate([0 * ones, -sin, zeros, tail], axis=1)
    sin_hi = jnp.concatenate([0 * ones, zeros, sin, tail], axis=1)
    return cos_t, sin_lo, sin_hi


def _pad_heads(w, heads, width, lo, hi):
    k = w.shape[0]
    w = w.reshape(k, heads, width)[:, :, lo:hi]
    w = jnp.pad(w, ((0, 0), (0, 0), (0, LANES - (hi - lo))))
    return w.reshape(k, heads * LANES)


def _mixer_proj(h3, gain, w_in, q_norm, w_uq, kv_norm, w_ukv, tables, *, blk):
    b, t, d = h3.shape
    assert t % blk == 0 and blk % LANES == 0
    widths = dict(g_a=256, u_a=256, q_d=512, k_d=512, v_d=512, c_q=w_uq.shape[0], c_kv=w_ukv.shape[0],
                  k_r=MLA_ROPE_DIM)
    offs, lo = {}, 0
    for name, wd in widths.items():
        offs[name] = (lo, lo + wd)
        lo += wd
    assert lo == w_in.shape[1]
    kr_lo, kr_hi = offs["k_r"]
    w_kr = jnp.pad(w_in[:, kr_lo:kr_hi], ((0, 0), (MLA_NOPE_DIM, LANES - MLA_NOPE_DIM - MLA_ROPE_DIM)))
    w_all = jnp.concatenate([w_in[:, :kr_lo], w_kr], axis=1).astype(BF16)
    offs["k_r"] = (kr_lo, kr_lo + LANES)
    v_lo, v_hi = offs["v_d"]
    w_vt = w_in[:, v_lo:v_hi].T.astype(BF16)
    qk = MLA_NOPE_DIM + MLA_ROPE_DIM
    w_uq_p = _pad_heads(w_uq, MLA_HEADS, qk, 0, qk).astype(BF16)
    w_uk_p = _pad_heads(w_ukv, MLA_HEADS, MLA_NOPE_DIM + MLA_V_DIM, 0, MLA_NOPE_DIM).astype(BF16)
    w_uv = w_ukv.reshape(-1, MLA_HEADS, MLA_NOPE_DIM + MLA_V_DIM)[:, :, MLA_NOPE_DIM:]
    w_uvt = w_uv.reshape(-1, MLA_HEADS * MLA_V_DIM).T.astype(BF16)

    def rows(width, dtype):
        return (pl.BlockSpec((None, blk, width), lambda i, j: (i, j, 0)),
                jax.ShapeDtypeStruct((b, t, width), dtype))

    def cols(width):
        return (pl.BlockSpec((None, None, width, blk), lambda i, j: (i, j, 0, 0)),
                jax.ShapeDtypeStruct((b, t // blk, width, blk), BF16))

    outs = [rows(256, F32), rows(256, F32), rows(512, BF16), rows(512, BF16), cols(512),
            rows(MLA_HEADS * LANES, BF16), rows(MLA_HEADS * LANES, BF16), cols(MLA_HEADS * MLA_V_DIM)]
    table_spec = pl.BlockSpec((blk, LANES), lambda i, j: (j, 0))
    return pl.pallas_call(
        functools.partial(_proj_kernel, offs=offs),
        grid=(b, t // blk),
        in_specs=[pl.BlockSpec((None, blk, d), lambda i, j: (i, j, 0)), _resident((1, d)),
                  _resident(w_all.shape), _resident(w_vt.shape), _resident((1, w_uq.shape[0])),
                  _resident(w_uq_p.shape), _resident((1, w_ukv.shape[0])), _resident(w_uk_p.shape),
                  _resident(w_uvt.shape), table_spec, table_spec, table_spec],
        out_specs=[o[0] for o in outs],
        out_shape=[o[1] for o in outs],
        compiler_params=_params("parallel", "parallel"),
        name="mixer_proj",
    )(h3, gain.reshape(1, d), w_all, w_vt, q_norm.reshape(1, -1), w_uq_p, kv_norm.reshape(1, -1), w_uk_p, w_uvt,
      *tables)


def _lru_kernel(g_ref, u_ref, cw_ref, cb_ref, wa_ref, ba_ref, wx_ref, bx_ref, lam_ref, on_ref, halo0_ref, carry0_ref,
                y_ref, halo_out_ref, carry_out_ref, halo_ref, carry_ref, a_scr, b_scr, h_scr, *, chunk):
    @pl.when(pl.program_id(1) == 0)
    def _():
        halo_ref[...] = halo0_ref[...]
        carry_ref[...] = carry0_ref[...]

    u = u_ref[...]
    ext = jnp.concatenate([halo_ref[...], u], axis=0)
    halo_ref[...] = u[chunk - SUBLANES:, :]
    conv = cb_ref[...] + cw_ref[CONV_WIDTH - 1:CONV_WIDTH, :] * u
    for j in range(CONV_WIDTH - 1):
        shifted = pltpu.roll(ext, CONV_WIDTH - 1 - j, 0)[SUBLANES:, :]
        conv = conv + cw_ref[j:j + 1, :] * shifted

    ub = conv.astype(BF16)
    r = jax.nn.sigmoid(jnp.dot(ub, wa_ref[...], preferred_element_type=F32) + ba_ref[...])
    i = jax.nn.sigmoid(jnp.dot(ub, wx_ref[...], preferred_element_type=F32) + bx_ref[...])
    neg_lam = -lam_ref[...]
    softplus = jnp.maximum(neg_lam, 0.0) + jnp.log1p(jnp.exp(-jnp.abs(neg_lam)))
    log_a = -LRU_C * r * softplus
    a = jnp.exp(log_a)
    gain2 = -jnp.tanh(log_a) * (a * a + 1.0)
    b = jnp.where(gain2 > 0.0, gain2 * lax.rsqrt(gain2), 0.0) * (i * conv)

    groups = (chunk // SUBLANES, SUBLANES, a.shape[-1])
    a, b = a.reshape(groups), b.reshape(groups)
    row = lax.broadcasted_iota(jnp.int32, groups, 1)
    for s in (1, 2, 4):
        keep = row >= s
        b = jnp.where(keep, a * pltpu.roll(b, s, 1) + b, b)
        a = jnp.where(keep, a * pltpu.roll(a, s, 1), a)
    a_scr[...] = a.reshape(chunk, -1)
    b_scr[...] = b.reshape(chunk, -1)

    def group(gi, carry):
        sl = pl.ds(pl.multiple_of(gi * SUBLANES, SUBLANES), SUBLANES)
        h = a_scr[sl, :] * carry + b_scr[sl, :]
        h_scr[sl, :] = h
        return jnp.broadcast_to(h[SUBLANES - 1:SUBLANES, :], h.shape)

    carry = lax.fori_loop(0, chunk // SUBLANES, group, carry_ref[...])
    carry_ref[...] = carry
    carry_out_ref[...] = carry
    halo_out_ref[...] = u[chunk - SUBLANES:, :]
    y = h_scr[...] * jax.nn.gelu(g_ref[...])
    y_ref[...] = _rmsnorm(y, on_ref[...]).astype(BF16)


def _block_diag(w):
    n, c, d = w.shape
    eye = jnp.eye(n, dtype=w.dtype)
    return (w[:, :, None, :] * eye[:, None, :, None]).reshape(n * c, n * d)


def _lru(g, u, conv_w, conv_b, wa, ba, wx, bx, lam, out_norm, *, chunk, halo0=None, carry0=None):
    b, t, c = u.shape
    assert t % chunk == 0 and chunk % (2 * SUBLANES) == 0
    if halo0 is None:
        halo0 = carry0 = jnp.zeros((SUBLANES, c), F32)
    row = lambda v: v.reshape(1, c)
    seq = pl.BlockSpec((None, chunk, c), lambda i, j: (i, j, 0))
    ctx = pl.BlockSpec((None, None, SUBLANES, c), lambda i, j: (i, j, 0, 0))
    ctx_shape = jax.ShapeDtypeStruct((b, t // chunk, SUBLANES, c), F32)
    return pl.pallas_call(
        functools.partial(_lru_kernel, chunk=chunk),
        grid=(b, t // chunk),
        in_specs=[seq, seq, _resident((CONV_WIDTH, c)), _resident((1, c)), _resident((c, c)), _resident((1, c)),
                  _resident((c, c)), _resident((1, c)), _resident((1, c)), _resident((1, c)),
                  _resident((SUBLANES, c)), _resident((SUBLANES, c))],
        out_specs=[seq, ctx, ctx],
        out_shape=[jax.ShapeDtypeStruct((b, t, c), BF16), ctx_shape, ctx_shape],
        scratch_shapes=[pltpu.VMEM((SUBLANES, c), F32), pltpu.VMEM((SUBLANES, c), F32),
                        pltpu.VMEM((chunk, c), F32), pltpu.VMEM((chunk, c), F32), pltpu.VMEM((chunk, c), F32)],
        compiler_params=_params("parallel", "arbitrary"),
        name="rg_lru",
    )(g, u, conv_w, row(conv_b), _block_diag(wa).astype(BF16), row(ba), _block_diag(wx).astype(BF16), row(bx),
      row(lam), row(out_norm), halo0, carry0)


def _attn_kernel(q_ref, k_ref, vt_ref, *rest, blk, ratio, mode, groups, lam_init, has_prefix):
    if has_prefix:
        kpre_ref, vtpre_ref, *rest = rest
    if mode == "diff":
        lam_ref, subln_ref, o_ref, m_scr, l_scr, acc_scr, s_scr = rest
    else:
        o_ref, m_scr, l_scr, acc_scr, s_scr = rest
    qi = pl.program_id(2)
    width = LANES if mode == "diff" else 2 * LANES
    v_rows = LANES if mode == "diff" else LANES // 2
    chains = [(g, c) for g in range(groups) for c in range(2)]

    def k_lanes(i):
        g, c = chains[i]
        lo = g * width + (0 if mode == "diff" else c * LANES)
        return slice(lo, lo + LANES)

    def v_sublanes(i):
        lo = (i // 2) * LANES if mode == "diff" else i * v_rows
        return slice(lo, lo + v_rows)

    qs = []
    for g in range(groups):
        if mode == "diff":
            q = q_ref[:, g * width:(g + 1) * width]
            low = lax.broadcasted_iota(jnp.int32, q.shape, 1) < (LANES // 2)
            zero = jnp.zeros_like(q)
            qs += [jnp.where(low, q, zero), jnp.where(low, zero, q)]
        else:
            qs += [q_ref[:, g * width:g * width + LANES], q_ref[:, g * width + LANES:(g + 1) * width]]

    def scores(kj):
        rows = pl.ds(pl.multiple_of(kj * blk, blk), blk)
        return [lax.dot_general(k_ref[rows, k_lanes(i)], qs[i], _NT, preferred_element_type=F32)
                for i in range(len(chains))]

    def values(kj, i):
        return vt_ref[kj, v_sublanes(i), :]

    first = qi * ratio
    diag = [scores(first + r) for r in range(ratio)]
    if has_prefix:
        pre = [lax.dot_general(kpre_ref[:, k_lanes(i)], qs[i], _NT, preferred_element_type=F32)
               for i in range(len(chains))]
    for i, s in enumerate(scores(0)):
        s_scr[0, i] = s

    def causal(s, r):
        key = lax.broadcasted_iota(jnp.int32, s.shape, 0) + r * blk
        qry = lax.broadcasted_iota(jnp.int32, s.shape, 1)
        return jnp.where(key <= qry, s, -jnp.inf)

    ps = []
    for i in range(len(chains)):
        s = causal(diag[0][i], 0)
        m = jnp.max(s, axis=0, keepdims=True)
        if has_prefix:
            m = jnp.maximum(m, jnp.max(pre[i], axis=0, keepdims=True))
        p = jnp.exp2(s - m)
        l = jnp.sum(p, axis=0, keepdims=True)
        if has_prefix:
            p_pre = jnp.exp2(pre[i] - m)
            l = l + jnp.sum(p_pre, axis=0, keepdims=True)
            ps.append((p.astype(BF16), p_pre.astype(BF16)))
        else:
            ps.append((p.astype(BF16), None))
        m_scr[i] = m
        l_scr[i] = l
    for i in range(len(chains)):
        p, p_pre = ps[i]
        acc = jnp.dot(values(first, i), p, preferred_element_type=F32)
        if has_prefix:
            acc = acc + jnp.dot(vtpre_ref[v_sublanes(i), :], p_pre, preferred_element_type=F32)
        acc_scr[i] = acc

    def accumulate(kj, ss):
        ps = []
        for i in range(len(chains)):
            s = ss[i]
            m_old = m_scr[i]
            m_new = jnp.maximum(m_old, jnp.max(s, axis=0, keepdims=True))
            alpha = jnp.exp2(m_old - m_new)
            p = jnp.exp2(s - m_new)
            l_scr[i] = alpha * l_scr[i] + jnp.sum(p, axis=0, keepdims=True)
            m_scr[i] = m_new
            ps.append((alpha, p.astype(BF16)))
        for i in range(len(chains)):
            alpha, p = ps[i]
            acc_scr[i] = alpha * acc_scr[i] + jnp.dot(values(kj, i), p, preferred_element_type=F32)

    for r in range(1, ratio):
        accumulate(first + r, [causal(s, r) for s in diag[r]])

    def buffered(slot):
        return [s_scr[slot, i] for i in range(len(chains))]

    def pair(i, carry):
        for j, s in enumerate(scores(2 * i + 1)):
            s_scr[1, j] = s
        accumulate(2 * i, buffered(0))
        for j, s in enumerate(scores(jnp.minimum(2 * i + 2, first))):
            s_scr[0, j] = s
        accumulate(2 * i + 1, buffered(1))
        return carry

    lax.fori_loop(0, first // 2, pair, 0)

    if ratio % 2 == 1:
        @pl.when(first % 2 == 1)
        def _():
            accumulate(first - 1, buffered(0))

    if mode == "diff":
        p_ = lam_ref[...]
        lam = (jnp.exp(jnp.sum(p_[0:1] * p_[1:2], axis=-1, keepdims=True))
               - jnp.exp(jnp.sum(p_[2:3] * p_[3:4], axis=-1, keepdims=True)) + lam_init)
    for g in range(groups):
        o_a = acc_scr[2 * g] * (1.0 / l_scr[2 * g])
        o_b = acc_scr[2 * g + 1] * (1.0 / l_scr[2 * g + 1])
        cols = slice(g * LANES, (g + 1) * LANES)
        if mode == "diff":
            o = (o_a - lam * o_b).T
            o_ref[:, cols] = (_rmsnorm(o, subln_ref[...]) * (1.0 - lam_init)).astype(o_ref.dtype)
        else:
            o_ref[:, cols] = jnp.concatenate([o_a, o_b], axis=0).T.astype(o_ref.dtype)


def _attention(q, k, vt, *, mode, prefix=None, lam_params=None, subln=None, lam_init=0.0, out_dtype=BF16,
               groups_per_step=2):
    b, t, _ = q.shape
    blk = vt.shape[3]
    width = LANES if mode == "diff" else 2 * LANES
    total = vt.shape[2] // LANES
    groups = min(total, groups_per_step)
    assert t % blk == 0 and q.shape[2] == total * width and total % groups == 0
    ratio = 2 if (t // blk) % 2 == 0 else 1
    blk_q = ratio * blk
    in_specs = [pl.BlockSpec((None, blk_q, groups * width), lambda i, g, j: (i, j, g)),
                pl.BlockSpec((None, t, groups * width), lambda i, g, j: (i, 0, g)),
                pl.BlockSpec((None, t // blk, groups * LANES, blk), lambda i, g, j: (i, 0, g, 0))]
    args = [q, k, vt]
    if prefix is not None:
        k_pre, vt_pre = prefix
        n_pre = k_pre.shape[0]
        in_specs += [pl.BlockSpec((n_pre, groups * width), lambda i, g, j: (0, g)),
                     pl.BlockSpec((groups * LANES, n_pre), lambda i, g, j: (g, 0))]
        args += [k_pre, vt_pre]
    if mode == "diff":
        in_specs += [_resident(lam_params.shape), _resident((1, LANES))]
        args += [lam_params, subln.reshape(1, LANES)]
    chains = 2 * groups
    v_rows = LANES if mode == "diff" else LANES // 2
    return pl.pallas_call(
        functools.partial(_attn_kernel, blk=blk, ratio=ratio, mode=mode, groups=groups, lam_init=lam_init,
                          has_prefix=prefix is not None),
        grid=(b, total // groups, t // blk_q),
        in_specs=in_specs,
        out_specs=pl.BlockSpec((None, blk_q, groups * LANES), lambda i, g, j: (i, j, g)),
        out_shape=jax.ShapeDtypeStruct((b, t, total * LANES), out_dtype),
        scratch_shapes=[pltpu.VMEM((chains, 1, blk_q), F32), pltpu.VMEM((chains, 1, blk_q), F32),
                        pltpu.VMEM((chains, v_rows, blk_q), F32), pltpu.VMEM((2, chains, blk, blk_q), F32)],
        compiler_params=_params("parallel", "parallel", "arbitrary"),
        name="attn_" + mode,
    )(*args)


def kernel(x, meta_tokens, ffn1_norm, ffn1_in, ffn1_out, mix_norm, w_in, conv_w, conv_b, lru_wa, lru_ba, lru_wx, lru_bx, lru_lambda, lru_out_norm, lam_q1, lam_k1, lam_q2, lam_k2, diff_subln, q_norm, w_uq, kv_norm, w_ukv, mla_out_norm, w_out, ffn2_norm, ffn2_in, ffn2_out, final_norm):
    b, s, d = x.shape
    depth = ffn1_in.shape[0]
    t = -(-s // SEQ_BLOCK) * SEQ_BLOCK
    t_meta = LANES
    assert N_META % (2 * SUBLANES) == 0 and N_META <= t_meta
    h = jnp.pad(x, ((0, 0), (0, t - s), (0, 0))).reshape(b * t, d)
    h_meta = jnp.pad(meta_tokens.astype(x.dtype), ((0, t_meta - N_META), (0, 0)))
    tables = _rope_tables(N_META, t)
    tables_meta = _rope_tables(0, t_meta)
    lru_chunk = t // 4 if (t // 4) % (2 * SUBLANES) == 0 else SEQ_BLOCK

    def mixer(hs, l, *, seq, blk, chunk, rope, prefix):
        g_a, u_a, q_d, k_d, vt_d, q_c, k_c, vt_c = _mixer_proj(
            hs.reshape(-1, seq, d), mix_norm[l], w_in[l], q_norm[l], w_uq[l], kv_norm[l], w_ukv[l], rope, blk=blk)
        halo0, carry0, pre_d, pre_c = prefix if prefix is not None else (None,) * 4
        y_a, halo, carry = _lru(g_a, u_a, conv_w[l], conv_b[l], lru_wa[l], lru_ba[l], lru_wx[l], lru_bx[l],
                                lru_lambda[l], lru_out_norm[l], chunk=chunk, halo0=halo0, carry0=carry0)
        lam_init = 0.8 - 0.6 * math.exp(-0.3 * l)
        lam_params = jnp.stack([lam_q1[l], lam_k1[l], lam_q2[l], lam_k2[l]]).astype(F32)
        y_b = _attention(q_d, k_d, vt_d, mode="diff", prefix=pre_d, lam_params=lam_params, subln=diff_subln[l],
                         lam_init=lam_init)
        o_c = _attention(q_c, k_c, vt_c, mode="pair", prefix=pre_c, out_dtype=F32)
        export = (halo[0, 0], carry[0, 0], (k_d[0, :N_META], vt_d[0, 0, :, :N_META]),
                  (k_c[0, :N_META], vt_c[0, 0, :, :N_META]))
        n = hs.shape[0]
        return (y_a.reshape(n, -1), y_b.reshape(n, -1), o_c.reshape(n, -1)), export

    for l in range(depth):
        w1 = (ffn1_norm[l], ffn1_in[l].astype(BF16), ffn1_out[l].astype(BF16))
        w2 = (ffn2_norm[l], ffn2_in[l].astype(BF16), ffn2_out[l].astype(BF16))
        w_mix_out = w_out[l].astype(BF16)
        last = l == depth - 1
        h_meta = _ffn(h_meta, *w1, row_tile=t_meta)
        ys_meta, prefix = mixer(h_meta, l, seq=t_meta, blk=t_meta, chunk=N_META, rope=tables_meta, prefix=None)
        h = _ffn(h, *w1)
        ys, _ = mixer(h, l, seq=t, blk=SEQ_BLOCK, chunk=lru_chunk, rope=tables, prefix=prefix)
        h = _ffn(h, *w2, final_gain=final_norm if last else None, mixer=(*ys, mla_out_norm[l], w_mix_out))
        if not last:
            h_meta = _ffn(h_meta, *w2, mixer=(*ys_meta, mla_out_norm[l], w_mix_out), row_tile=t_meta)
    return h.reshape(b, t, d)[:, :s]
```

```python
import functools
import math

import jax
import jax.numpy as jnp
from jax import lax
from jax.experimental import pallas as pl
from jax.experimental.pallas import tpu as pltpu

F32 = jnp.float32
BF16 = jnp.bfloat16

N_META = 16
NORM_EPS = 1e-6
CONV_WIDTH = 4
LRU_C = 8.0
DIFF_QK_DIM = 64
MLA_HEADS = 4
MLA_NOPE_DIM = 64
MLA_ROPE_DIM = 32
MLA_V_DIM = 64
ROPE_THETA = 10000.0
LOG2_E = math.log2(math.e)

LANES = 128
SUBLANES = 8
MXU_DIM = 256
SEQ_BLOCK = MXU_DIM
VMEM_LIMIT_BYTES = 56 * 1024 * 1024

_NT = (((1,), (1,)), ((), ()))


def _params(*semantics):
    return pltpu.CompilerParams(dimension_semantics=semantics, vmem_limit_bytes=VMEM_LIMIT_BYTES)


def _resident(shape):
    zeros = (0,) * len(shape)
    return pl.BlockSpec(shape, lambda *_: zeros, pipeline_mode=pl.Buffered(1))


def _rmsnorm(x, gain):
    return x * lax.rsqrt(jnp.mean(x * x, axis=-1, keepdims=True) + NORM_EPS) * gain


def _mixer_out_rows(ya_ref, yb_ref, oc_ref, cn_ref, w_ref):
    a_hi = ya_ref.shape[1]
    b_hi = a_hi + yb_ref.shape[1]
    yc = _rmsnorm(oc_ref[...], cn_ref[...]).astype(BF16)
    acc = jnp.dot(ya_ref[...], w_ref[:a_hi, :], preferred_element_type=F32)
    acc += jnp.dot(yb_ref[...], w_ref[a_hi:b_hi, :], preferred_element_type=F32)
    return acc + jnp.dot(yc, w_ref[b_hi:, :], preferred_element_type=F32)


def _ffn_kernel(h_ref, gain_ref, w_in_ref, w_out_ref, *rest, d_ff, ff_chunk, final_norm, with_mixer):
    if with_mixer:
        mix_refs, rest = rest[:5], rest[5:]
    if final_norm:
        final_gain_ref, o_ref, acc_ref = rest
    else:
        o_ref, acc_ref = rest
    x = h_ref[...]
    if with_mixer:
        x = x + _mixer_out_rows(*mix_refs)
    xn = _rmsnorm(x, gain_ref[...]).astype(BF16)
    for j in range(d_ff // ff_chunk):
        lo = j * ff_chunk
        gate = jnp.dot(xn, w_in_ref[:, lo:lo + ff_chunk], preferred_element_type=F32)
        up = jnp.dot(xn, w_in_ref[:, d_ff + lo:d_ff + lo + ff_chunk], preferred_element_type=F32)
        act = (jax.nn.silu(gate) * up).astype(BF16)
        part = jnp.dot(act, w_out_ref[lo:lo + ff_chunk, :], preferred_element_type=F32)
        if j == 0:
            acc_ref[...] = part
        else:
            acc_ref[...] += part
    y = x + 0.5 * acc_ref[...]
    if final_norm:
        y = _rmsnorm(y, final_gain_ref[...])
    o_ref[...] = y


def _ffn(h, gain, w_in, w_out, final_gain=None, mixer=None, *, row_tile=512, ff_chunk=256):
    n, d = h.shape
    d_ff = w_out.shape[0]
    assert n % row_tile == 0 and d_ff % ff_chunk == 0
    final_norm = final_gain is not None
    rows = lambda width: pl.BlockSpec((row_tile, width), lambda i: (i, 0))
    in_specs = [rows(d), _resident((1, d)), _resident(w_in.shape), _resident(w_out.shape)]
    args = [h, gain.reshape(1, d), w_in, w_out]
    if mixer is not None:
        y_a, y_b, o_c, c_norm, w_mix = mixer
        assert y_a.shape[1] + y_b.shape[1] + o_c.shape[1] == w_mix.shape[0]
        in_specs += [rows(y_a.shape[1]), rows(y_b.shape[1]), rows(o_c.shape[1]), _resident((1, o_c.shape[1])),
                     _resident(w_mix.shape)]
        args += [y_a, y_b, o_c, c_norm.reshape(1, -1), w_mix]
    if final_norm:
        in_specs.append(_resident((1, d)))
        args.append(final_gain.reshape(1, d))
    return pl.pallas_call(
        functools.partial(_ffn_kernel, d_ff=d_ff, ff_chunk=ff_chunk, final_norm=final_norm,
                          with_mixer=mixer is not None),
        grid=(n // row_tile,),
        in_specs=in_specs,
        out_specs=pl.BlockSpec((row_tile, d), lambda i: (i, 0)),
        out_shape=jax.ShapeDtypeStruct((n, d), F32),
        scratch_shapes=[pltpu.VMEM((row_tile, d), F32)],
        compiler_params=_params("parallel"),
        name="ffn",
    )(*args)


def _rope(x, cos, sin_lo, sin_hi):
    half = MLA_ROPE_DIM // 2
    return x * cos + pltpu.roll(x, LANES - half, 1) * sin_lo + pltpu.roll(x, half, 1) * sin_hi


def _proj_kernel(h_ref, gain_ref, w_ref, wvt_ref, qn_ref, wuq_ref, kvn_ref, wuk_ref, wuvt_ref,
                 cos_ref, slo_ref, shi_ref,
                 g_ref, u_ref, qd_ref, kd_ref, vdt_ref, qc_ref, kc_ref, vct_ref, *, offs):
    xn = _rmsnorm(h_ref[...], gain_ref[...]).astype(BF16)

    def proj(name):
        lo, hi = offs[name]
        return jnp.dot(xn, w_ref[:, lo:hi], preferred_element_type=F32)

    cq_raw, ckv_raw, kr_raw = proj("c_q"), proj("c_kv"), proj("k_r")

    g_ref[...] = proj("g_a")
    u_ref[...] = proj("u_a")
    qd_ref[...] = (proj("q_d") * (DIFF_QK_DIM ** -0.5 * LOG2_E)).astype(BF16)
    kd_ref[...] = proj("k_d").astype(BF16)
    vdt_ref[...] = lax.dot_general(wvt_ref[...], xn, _NT, preferred_element_type=F32).astype(BF16)

    cos, slo, shi = cos_ref[...], slo_ref[...], shi_ref[...]
    cq = _rmsnorm(cq_raw, qn_ref[...]).astype(BF16)
    q = jnp.dot(cq, wuq_ref[...], preferred_element_type=F32)
    ckv = _rmsnorm(ckv_raw, kvn_ref[...]).astype(BF16)
    k_nope = jnp.dot(ckv, wuk_ref[...], preferred_element_type=F32)
    vct_ref[...] = lax.dot_general(wuvt_ref[...], ckv, _NT, preferred_element_type=F32).astype(BF16)
    k_rope = _rope(kr_raw, cos, slo, shi)
    q_scale = (MLA_NOPE_DIM + MLA_ROPE_DIM) ** -0.5 * LOG2_E
    for hd in range(MLA_HEADS):
        sl = slice(hd * LANES, (hd + 1) * LANES)
        qc_ref[:, sl] = (_rope(q[:, sl], cos, slo, shi) * q_scale).astype(BF16)
        kc_ref[:, sl] = (k_nope[:, sl] + k_rope).astype(BF16)


def _rope_tables(first_pos, t_len):
    half = MLA_ROPE_DIM // 2
    inv_freq = ROPE_THETA ** (-jnp.arange(half, dtype=F32) / half)
    pos = jnp.arange(first_pos, first_pos + t_len, dtype=jnp.int32)
    ang = pos.astype(F32)[:, None] * inv_freq[None, :]
    cos, sin = jnp.cos(ang), jnp.sin(ang)
    zeros = jnp.zeros((t_len, half), F32)
    ones = jnp.ones((t_len, MLA_NOPE_DIM), F32)
    tail = jnp.zeros((t_len, LANES - MLA_NOPE_DIM - MLA_ROPE_DIM), F32)
    cos_t = jnp.concatenate([ones, cos, cos, tail], axis=1)
    sin_lo = jnp.concatenate([0 * ones, -sin, zeros, tail], axis=1)
    sin_hi = jnp.concatenate([0 * ones, zeros, sin, tail], axis=1)
    return cos_t, sin_lo, sin_hi


def _pad_heads(w, heads, width, lo, hi):
    k = w.shape[0]
    w = w.reshape(k, heads, width)[:, :, lo:hi]
    w = jnp.pad(w, ((0, 0), (0, 0), (0, LANES - (hi - lo))))
    return w.reshape(k, heads * LANES)


def _mixer_proj(h3, gain, w_in, q_norm, w_uq, kv_norm, w_ukv, tables, *, blk):
    b, t, d = h3.shape
    assert t % blk == 0 and blk % LANES == 0
    widths = dict(g_a=256, u_a=256, q_d=512, k_d=512, v_d=512, c_q=w_uq.shape[0], c_kv=w_ukv.shape[0],
                  k_r=MLA_ROPE_DIM)
    offs, lo = {}, 0
    for name, wd in widths.items():
        offs[name] = (lo, lo + wd)
        lo += wd
    assert lo == w_in.shape[1]
    kr_lo, kr_hi = offs["k_r"]
    w_kr = jnp.pad(w_in[:, kr_lo:kr_hi], ((0, 0), (MLA_NOPE_DIM, LANES - MLA_NOPE_DIM - MLA_ROPE_DIM)))
    w_all = jnp.concatenate([w_in[:, :kr_lo], w_kr], axis=1).astype(BF16)
    offs["k_r"] = (kr_lo, kr_lo + LANES)
    v_lo, v_hi = offs["v_d"]
    w_vt = w_in[:, v_lo:v_hi].T.astype(BF16)
    qk = MLA_NOPE_DIM + MLA_ROPE_DIM
    w_uq_p = _pad_heads(w_uq, MLA_HEADS, qk, 0, qk).astype(BF16)
    w_uk_p = _pad_heads(w_ukv, MLA_HEADS, MLA_NOPE_DIM + MLA_V_DIM, 0, MLA_NOPE_DIM).astype(BF16)
    w_uv = w_ukv.reshape(-1, MLA_HEADS, MLA_NOPE_DIM + MLA_V_DIM)[:, :, MLA_NOPE_DIM:]
    w_uvt = w_uv.reshape(-1, MLA_HEADS * MLA_V_DIM).T.astype(BF16)

    def rows(width, dtype):
        return (pl.BlockSpec((None, blk, width), lambda i, j: (i, j, 0)),
                jax.ShapeDtypeStruct((b, t, width), dtype))

    def cols(width):
        return (pl.BlockSpec((None, None, width, blk), lambda i, j: (i, j, 0, 0)),
                jax.ShapeDtypeStruct((b, t // blk, width, blk), BF16))

    outs = [rows(256, F32), rows(256, F32), rows(512, BF16), rows(512, BF16), cols(512),
            rows(MLA_HEADS * LANES, BF16), rows(MLA_HEADS * LANES, BF16), cols(MLA_HEADS * MLA_V_DIM)]
    table_spec = pl.BlockSpec((blk, LANES), lambda i, j: (j, 0))
    return pl.pallas_call(
        functools.partial(_proj_kernel, offs=offs),
        grid=(b, t // blk),
        in_specs=[pl.BlockSpec((None, blk, d), lambda i, j: (i, j, 0)), _resident((1, d)),
                  _resident(w_all.shape), _resident(w_vt.shape), _resident((1, w_uq.shape[0])),
                  _resident(w_uq_p.shape), _resident((1, w_ukv.shape[0])), _resident(w_uk_p.shape),
                  _resident(w_uvt.shape), table_spec, table_spec, table_spec],
        out_specs=[o[0] for o in outs],
        out_shape=[o[1] for o in outs],
        compiler_params=_params("parallel", "parallel"),
        name="mixer_proj",
    )(h3, gain.reshape(1, d), w_all, w_vt, q_norm.reshape(1, -1), w_uq_p, kv_norm.reshape(1, -1), w_uk_p, w_uvt,
      *tables)


def _lru_kernel(g_ref, u_ref, cw_ref, cb_ref, wa_ref, ba_ref, wx_ref, bx_ref, lam_ref, on_ref, halo0_ref, carry0_ref,
                y_ref, halo_out_ref, carry_out_ref, halo_ref, carry_ref, a_scr, b_scr, h_scr, *, chunk):
    @pl.when(pl.program_id(1) == 0)
    def _():
        halo_ref[...] = halo0_ref[...]
        carry_ref[...] = carry0_ref[...]

    u = u_ref[...]
    ext = jnp.concatenate([halo_ref[...], u], axis=0)
    halo_ref[...] = u[chunk - SUBLANES:, :]
    conv = cb_ref[...] + cw_ref[CONV_WIDTH - 1:CONV_WIDTH, :] * u
    for j in range(CONV_WIDTH - 1):
        shifted = pltpu.roll(ext, CONV_WIDTH - 1 - j, 0)[SUBLANES:, :]
        conv = conv + cw_ref[j:j + 1, :] * shifted

    ub = conv.astype(BF16)
    r = jax.nn.sigmoid(jnp.dot(ub, wa_ref[...], preferred_element_type=F32) + ba_ref[...])
    i = jax.nn.sigmoid(jnp.dot(ub, wx_ref[...], preferred_element_type=F32) + bx_ref[...])
    neg_lam = -lam_ref[...]
    softplus = jnp.maximum(neg_lam, 0.0) + jnp.log1p(jnp.exp(-jnp.abs(neg_lam)))
    log_a = -LRU_C * r * softplus
    a = jnp.exp(log_a)
    gain2 = -jnp.tanh(log_a) * (a * a + 1.0)
    b = jnp.where(gain2 > 0.0, gain2 * lax.rsqrt(gain2), 0.0) * (i * conv)

    groups = (chunk // SUBLANES, SUBLANES, a.shape[-1])
    a, b = a.reshape(groups), b.reshape(groups)
    row = lax.broadcasted_iota(jnp.int32, groups, 1)
    for s in (1, 2, 4):
        keep = row >= s
        b = jnp.where(keep, a * pltpu.roll(b, s, 1) + b, b)
        a = jnp.where(keep, a * pltpu.roll(a, s, 1), a)
    a_scr[...] = a.reshape(chunk, -1)
    b_scr[...] = b.reshape(chunk, -1)

    def group(gi, carry):
        sl = pl.ds(pl.multiple_of(gi * SUBLANES, SUBLANES), SUBLANES)
        h = a_scr[sl, :] * carry + b_scr[sl, :]
        h_scr[sl, :] = h
        return jnp.broadcast_to(h[SUBLANES - 1:SUBLANES, :], h.shape)

    carry = lax.fori_loop(0, chunk // SUBLANES, group, carry_ref[...])
    carry_ref[...] = carry
    carry_out_ref[...] = carry
    halo_out_ref[...] = u[chunk - SUBLANES:, :]
    y = h_scr[...] * jax.nn.gelu(g_ref[...])
    y_ref[...] = _rmsnorm(y, on_ref[...]).astype(BF16)


def _block_diag(w):
    n, c, d = w.shape
    eye = jnp.eye(n, dtype=w.dtype)
    return (w[:, :, None, :] * eye[:, None, :, None]).reshape(n * c, n * d)


def _lru(g, u, conv_w, conv_b, wa, ba, wx, bx, lam, out_norm, *, chunk, halo0=None, carry0=None):
    b, t, c = u.shape
    assert t % chunk == 0 and chunk % (2 * SUBLANES) == 0
    if halo0 is None:
        halo0 = carry0 = jnp.zeros((SUBLANES, c), F32)
    row = lambda v: v.reshape(1, c)
    seq = pl.BlockSpec((None, chunk, c), lambda i, j: (i, j, 0))
    ctx = pl.BlockSpec((None, None, SUBLANES, c), lambda i, j: (i, j, 0, 0))
    ctx_shape = jax.ShapeDtypeStruct((b, t // chunk, SUBLANES, c), F32)
    return pl.pallas_call(
        functools.partial(_lru_kernel, chunk=chunk),
        grid=(b, t // chunk),
        in_specs=[seq, seq, _resident((CONV_WIDTH, c)), _resident((1, c)), _resident((c, c)), _resident((1, c)),
                  _resident((c, c)), _resident((1, c)), _resident((1, c)), _resident((1, c)),
                  _resident((SUBLANES, c)), _resident((SUBLANES, c))],
        out_specs=[seq, ctx, ctx],
        out_shape=[jax.ShapeDtypeStruct((b, t, c), BF16), ctx_shape, ctx_shape],
        scratch_shapes=[pltpu.VMEM((SUBLANES, c), F32), pltpu.VMEM((SUBLANES, c), F32),
                        pltpu.VMEM((chunk, c), F32), pltpu.VMEM((chunk, c), F32), pltpu.VMEM((chunk, c), F32)],
        compiler_params=_params("parallel", "arbitrary"),
        name="rg_lru",
    )(g, u, conv_w, row(conv_b), _block_diag(wa).astype(BF16), row(ba), _block_diag(wx).astype(BF16), row(bx),
      row(lam), row(out_norm), halo0, carry0)


def _attn_kernel(q_ref, k_ref, vt_ref, *rest, blk, ratio, mode, groups, lam_init, has_prefix):
    if has_prefix:
        kpre_ref, vtpre_ref, *rest = rest
    if mode == "diff":
        lam_ref, subln_ref, o_ref, m_scr, l_scr, acc_scr, s_scr, bm_scr = rest
    else:
        o_ref, m_scr, l_scr, acc_scr, s_scr, bm_scr = rest
    qi = pl.program_id(2)
    width = LANES if mode == "diff" else 2 * LANES
    v_rows = LANES if mode == "diff" else LANES // 2
    chains = [(g, c) for g in range(groups) for c in range(2)]

    def k_lanes(i):
        g, c = chains[i]
        lo = g * width + (0 if mode == "diff" else c * LANES)
        return slice(lo, lo + LANES)

    def v_sublanes(i):
        lo = (i // 2) * LANES if mode == "diff" else i * v_rows
        return slice(lo, lo + v_rows)

    qs = []
    for g in range(groups):
        if mode == "diff":
            q = q_ref[:, g * width:(g + 1) * width]
            low = lax.broadcasted_iota(jnp.int32, q.shape, 1) < (LANES // 2)
            zero = jnp.zeros_like(q)
            qs += [jnp.where(low, q, zero), jnp.where(low, zero, q)]
        else:
            qs += [q_ref[:, g * width:g * width + LANES], q_ref[:, g * width + LANES:(g + 1) * width]]

    def scores(kj):
        rows = pl.ds(pl.multiple_of(kj * blk, blk), blk)
        return [lax.dot_general(k_ref[rows, k_lanes(i)], qs[i], _NT, preferred_element_type=F32)
                for i in range(len(chains))]

    def values(kj, i):
        return vt_ref[kj, v_sublanes(i), :]

    first = qi * ratio

    def produce(kj, slot, diag_r=None):
        for i, s in enumerate(scores(kj)):
            if diag_r is not None:
                key = lax.broadcasted_iota(jnp.int32, s.shape, 0) + diag_r * blk
                qry = lax.broadcasted_iota(jnp.int32, s.shape, 1)
                s = jnp.where(key <= qry, s, -jnp.inf)
            s_scr[slot, i] = s
            bm_scr[slot, i] = jnp.max(s, axis=0, keepdims=True)

    def consume(kj, slot):
        for i in range(len(chains)):
            m_old = m_scr[i]
            m_new = jnp.maximum(m_old, bm_scr[slot, i])
            alpha = jnp.exp2(m_old - m_new)
            p = jnp.exp2(s_scr[slot, i] - m_new)
            l_scr[i] = alpha * l_scr[i] + jnp.sum(p, axis=0, keepdims=True)
            m_scr[i] = m_new
            acc_scr[i] = alpha * acc_scr[i] + jnp.dot(values(kj, i), p.astype(BF16), preferred_element_type=F32)

    for r in range(ratio):
        produce(first + r, 2 + r, diag_r=r)
    if has_prefix:
        pre = [lax.dot_general(kpre_ref[:, k_lanes(i)], qs[i], _NT, preferred_element_type=F32)
               for i in range(len(chains))]
    produce(0, 0)

    for i in range(len(chains)):
        m = bm_scr[2, i]
        if has_prefix:
            m = jnp.maximum(m, jnp.max(pre[i], axis=0, keepdims=True))
        p = jnp.exp2(s_scr[2, i] - m)
        l = jnp.sum(p, axis=0, keepdims=True)
        acc = jnp.dot(values(first, i), p.astype(BF16), preferred_element_type=F32)
        if has_prefix:
            p_pre = jnp.exp2(pre[i] - m)
            l = l + jnp.sum(p_pre, axis=0, keepdims=True)
            acc = acc + jnp.dot(vtpre_ref[v_sublanes(i), :], p_pre.astype(BF16), preferred_element_type=F32)
        m_scr[i] = m
        l_scr[i] = l
        acc_scr[i] = acc
    for r in range(1, ratio):
        consume(first + r, 2 + r)

    def pair(i, carry):
        produce(2 * i + 1, 1)
        consume(2 * i, 0)
        produce(jnp.minimum(2 * i + 2, first), 0)
        consume(2 * i + 1, 1)
        return carry

    lax.fori_loop(0, first // 2, pair, 0)

    if ratio % 2 == 1:
        @pl.when(first % 2 == 1)
        def _():
            consume(first - 1, 0)

    if mode == "diff":
        p_ = lam_ref[...]
        lam = (jnp.exp(jnp.sum(p_[0:1] * p_[1:2], axis=-1, keepdims=True))
               - jnp.exp(jnp.sum(p_[2:3] * p_[3:4], axis=-1, keepdims=True)) + lam_init)
    for g in range(groups):
        o_a = acc_scr[2 * g] * (1.0 / l_scr[2 * g])
        o_b = acc_scr[2 * g + 1] * (1.0 / l_scr[2 * g + 1])
        cols = slice(g * LANES, (g + 1) * LANES)
        if mode == "diff":
            o = o_a - lam * o_b
            o = o * lax.rsqrt(jnp.mean(o * o, axis=0, keepdims=True) + NORM_EPS) * subln_ref[...]
            o_ref[:, cols] = (o * (1.0 - lam_init)).T.astype(o_ref.dtype)
        else:
            o_ref[:, cols] = jnp.concatenate([o_a, o_b], axis=0).T.astype(o_ref.dtype)


def _attention(q, k, vt, *, mode, prefix=None, lam_params=None, subln=None, lam_init=0.0, out_dtype=BF16,
               groups_per_step=2):
    b, t, _ = q.shape
    blk = vt.shape[3]
    width = LANES if mode == "diff" else 2 * LANES
    total = vt.shape[2] // LANES
    groups = min(total, groups_per_step)
    assert t % blk == 0 and q.shape[2] == total * width and total % groups == 0
    ratio = 2 if (t // blk) % 2 == 0 else 1
    blk_q = ratio * blk
    in_specs = [pl.BlockSpec((None, blk_q, groups * width), lambda i, g, j: (i, j, g)),
                pl.BlockSpec((None, t, groups * width), lambda i, g, j: (i, 0, g)),
                pl.BlockSpec((None, t // blk, groups * LANES, blk), lambda i, g, j: (i, 0, g, 0))]
    args = [q, k, vt]
    if prefix is not None:
        k_pre, vt_pre = prefix
        n_pre = k_pre.shape[0]
        in_specs += [pl.BlockSpec((n_pre, groups * width), lambda i, g, j: (0, g)),
                     pl.BlockSpec((groups * LANES, n_pre), lambda i, g, j: (g, 0))]
        args += [k_pre, vt_pre]
    if mode == "diff":
        in_specs += [_resident(lam_params.shape), _resident((LANES, blk_q))]
        args += [lam_params, jnp.broadcast_to(subln.reshape(LANES, 1), (LANES, blk_q))]
    chains = 2 * groups
    v_rows = LANES if mode == "diff" else LANES // 2
    return pl.pallas_call(
        functools.partial(_attn_kernel, blk=blk, ratio=ratio, mode=mode, groups=groups, lam_init=lam_init,
                          has_prefix=prefix is not None),
        grid=(b, total // groups, t // blk_q),
        in_specs=in_specs,
        out_specs=pl.BlockSpec((None, blk_q, groups * LANES), lambda i, g, j: (i, j, g)),
        out_shape=jax.ShapeDtypeStruct((b, t, total * LANES), out_dtype),
        scratch_shapes=[pltpu.VMEM((chains, 1, blk_q), F32), pltpu.VMEM((chains, 1, blk_q), F32),
                        pltpu.VMEM((chains, v_rows, blk_q), F32),
                        pltpu.VMEM((2 + ratio, chains, blk, blk_q), F32),
                        pltpu.VMEM((2 + ratio, chains, 1, blk_q), F32)],
        compiler_params=_params("parallel", "parallel", "arbitrary"),
        name="attn_" + mode,
    )(*args)


def kernel(x, meta_tokens, ffn1_norm, ffn1_in, ffn1_out, mix_norm, w_in, conv_w, conv_b, lru_wa, lru_ba, lru_wx, lru_bx, lru_lambda, lru_out_norm, lam_q1, lam_k1, lam_q2, lam_k2, diff_subln, q_norm, w_uq, kv_norm, w_ukv, mla_out_norm, w_out, ffn2_norm, ffn2_in, ffn2_out, final_norm):
    b, s, d = x.shape
    depth = ffn1_in.shape[0]
    t = -(-s // SEQ_BLOCK) * SEQ_BLOCK
    t_meta = LANES
    assert N_META % (2 * SUBLANES) == 0 and N_META <= t_meta
    h = jnp.pad(x, ((0, 0), (0, t - s), (0, 0))).reshape(b * t, d)
    h_meta = jnp.pad(meta_tokens.astype(x.dtype), ((0, t_meta - N_META), (0, 0)))
    tables = _rope_tables(N_META, t)
    tables_meta = _rope_tables(0, t_meta)
    lru_chunk = t // 4 if (t // 4) % (2 * SUBLANES) == 0 else SEQ_BLOCK

    def mixer(hs, l, *, seq, blk, chunk, rope, prefix):
        g_a, u_a, q_d, k_d, vt_d, q_c, k_c, vt_c = _mixer_proj(
            hs.reshape(-1, seq, d), mix_norm[l], w_in[l], q_norm[l], w_uq[l], kv_norm[l], w_ukv[l], rope, blk=blk)
        halo0, carry0, pre_d, pre_c = prefix if prefix is not None else (None,) * 4
        y_a, halo, carry = _lru(g_a, u_a, conv_w[l], conv_b[l], lru_wa[l], lru_ba[l], lru_wx[l], lru_bx[l],
                                lru_lambda[l], lru_out_norm[l], chunk=chunk, halo0=halo0, carry0=carry0)
        lam_init = 0.8 - 0.6 * math.exp(-0.3 * l)
        lam_params = jnp.stack([lam_q1[l], lam_k1[l], lam_q2[l], lam_k2[l]]).astype(F32)
        y_b = _attention(q_d, k_d, vt_d, mode="diff", prefix=pre_d, lam_params=lam_params, subln=diff_subln[l],
                         lam_init=lam_init)
        o_c = _attention(q_c, k_c, vt_c, mode="pair", prefix=pre_c, out_dtype=F32)
        export = (halo[0, 0], carry[0, 0], (k_d[0, :N_META], vt_d[0, 0, :, :N_META]),
                  (k_c[0, :N_META], vt_c[0, 0, :, :N_META]))
        n = hs.shape[0]
        return (y_a.reshape(n, -1), y_b.reshape(n, -1), o_c.reshape(n, -1)), export

    for l in range(depth):
        w1 = (ffn1_norm[l], ffn1_in[l].astype(BF16), ffn1_out[l].astype(BF16))
        w2 = (ffn2_norm[l], ffn2_in[l].astype(BF16), ffn2_out[l].astype(BF16))
        w_mix_out = w_out[l].astype(BF16)
        last = l == depth - 1
        h_meta = _ffn(h_meta, *w1, row_tile=t_meta)
        ys_meta, prefix = mixer(h_meta, l, seq=t_meta, blk=t_meta, chunk=N_META, rope=tables_meta, prefix=None)
        h = _ffn(h, *w1)
        ys, _ = mixer(h, l, seq=t, blk=SEQ_BLOCK, chunk=lru_chunk, rope=tables, prefix=prefix)
        h = _ffn(h, *w2, final_gain=final_norm if last else None, mixer=(*ys, mla_out_norm[l], w_mix_out))
        if not last:
            h_meta = _ffn(h_meta, *w2, mixer=(*ys_meta, mla_out_norm[l], w_mix_out), row_tile=t_meta)
    return h.reshape(b, t, d)[:, :s]
```

```python
import functools
import math

import jax
import jax.numpy as jnp
from jax import lax
from jax.experimental import pallas as pl
from jax.experimental.pallas import tpu as pltpu

F32 = jnp.float32
BF16 = jnp.bfloat16

N_META = 16
NORM_EPS = 1e-6
CONV_WIDTH = 4
LRU_C = 8.0
DIFF_QK_DIM = 64
MLA_HEADS = 4
MLA_NOPE_DIM = 64
MLA_ROPE_DIM = 32
MLA_V_DIM = 64
ROPE_THETA = 10000.0
LOG2_E = math.log2(math.e)

LANES = 128
SUBLANES = 8
MXU_DIM = 256
SEQ_BLOCK = MXU_DIM
VMEM_LIMIT_BYTES = 56 * 1024 * 1024

_NT = (((1,), (1,)), ((), ()))


def _params(*semantics):
    return pltpu.CompilerParams(dimension_semantics=semantics, vmem_limit_bytes=VMEM_LIMIT_BYTES)


def _resident(shape):
    zeros = (0,) * len(shape)
    return pl.BlockSpec(shape, lambda *_: zeros, pipeline_mode=pl.Buffered(1))


def _rmsnorm(x, gain):
    return x * lax.rsqrt(jnp.mean(x * x, axis=-1, keepdims=True) + NORM_EPS) * gain


def _mixer_out_rows(ya_ref, yb_ref, oc_ref, cn_ref, w_ref):
    a_hi = ya_ref.shape[1]
    b_hi = a_hi + yb_ref.shape[1]
    yc = _rmsnorm(oc_ref[...], cn_ref[...]).astype(BF16)
    acc = jnp.dot(ya_ref[...], w_ref[:a_hi, :], preferred_element_type=F32)
    acc += jnp.dot(yb_ref[...], w_ref[a_hi:b_hi, :], preferred_element_type=F32)
    return acc + jnp.dot(yc, w_ref[b_hi:, :], preferred_element_type=F32)


def _ffn_kernel(h_ref, gain_ref, w_in_ref, w_out_ref, *rest, d_ff, ff_chunk, final_norm, with_mixer):
    if with_mixer:
        mix_refs, rest = rest[:5], rest[5:]
    if final_norm:
        final_gain_ref, o_ref, acc_ref = rest
    else:
        o_ref, acc_ref = rest
    x = h_ref[...]
    if with_mixer:
        x = x + _mixer_out_rows(*mix_refs)
    xn = _rmsnorm(x, gain_ref[...]).astype(BF16)
    for j in range(d_ff // ff_chunk):
        lo = j * ff_chunk
        gate = jnp.dot(xn, w_in_ref[:, lo:lo + ff_chunk], preferred_element_type=F32)
        up = jnp.dot(xn, w_in_ref[:, d_ff + lo:d_ff + lo + ff_chunk], preferred_element_type=F32)
        act = (jax.nn.silu(gate) * up).astype(BF16)
        part = jnp.dot(act, w_out_ref[lo:lo + ff_chunk, :], preferred_element_type=F32)
        if j == 0:
            acc_ref[...] = part
        else:
            acc_ref[...] += part
    y = x + 0.5 * acc_ref[...]
    if final_norm:
        y = _rmsnorm(y, final_gain_ref[...])
    o_ref[...] = y


def _ffn(h, gain, w_in, w_out, final_gain=None, mixer=None, *, row_tile=512, ff_chunk=256):
    n, d = h.shape
    d_ff = w_out.shape[0]
    assert n % row_tile == 0 and d_ff % ff_chunk == 0
    final_norm = final_gain is not None
    rows = lambda width: pl.BlockSpec((row_tile, width), lambda i: (i, 0))
    in_specs = [rows(d), _resident((1, d)), _resident(w_in.shape), _resident(w_out.shape)]
    args = [h, gain.reshape(1, d), w_in, w_out]
    if mixer is not None:
        y_a, y_b, o_c, c_norm, w_mix = mixer
        assert y_a.shape[1] + y_b.shape[1] + o_c.shape[1] == w_mix.shape[0]
        in_specs += [rows(y_a.shape[1]), rows(y_b.shape[1]), rows(o_c.shape[1]), _resident((1, o_c.shape[1])),
                     _resident(w_mix.shape)]
        args += [y_a, y_b, o_c, c_norm.reshape(1, -1), w_mix]
    if final_norm:
        in_specs.append(_resident((1, d)))
        args.append(final_gain.reshape(1, d))
    return pl.pallas_call(
        functools.partial(_ffn_kernel, d_ff=d_ff, ff_chunk=ff_chunk, final_norm=final_norm,
                          with_mixer=mixer is not None),
        grid=(n // row_tile,),
        in_specs=in_specs,
        out_specs=pl.BlockSpec((row_tile, d), lambda i: (i, 0)),
        out_shape=jax.ShapeDtypeStruct((n, d), F32),
        scratch_shapes=[pltpu.VMEM((row_tile, d), F32)],
        compiler_params=_params("parallel"),
        name="ffn",
    )(*args)


def _rope(x, cos, sin_lo, sin_hi):
    half = MLA_ROPE_DIM // 2
    return x * cos + pltpu.roll(x, LANES - half, 1) * sin_lo + pltpu.roll(x, half, 1) * sin_hi


def _proj_kernel(h_ref, gain_ref, w_ref, wvt_ref, qn_ref, wuq_ref, kvn_ref, wuk_ref, wuvt_ref,
                 cos_ref, slo_ref, shi_ref,
                 g_ref, u_ref, qd_ref, kd_ref, vdt_ref, qc_ref, kc_ref, vct_ref, *, offs):
    xn = _rmsnorm(h_ref[...], gain_ref[...]).astype(BF16)

    def proj(name):
        lo, hi = offs[name]
        return jnp.dot(xn, w_ref[:, lo:hi], preferred_element_type=F32)

    cq_raw, ckv_raw, kr_raw = proj("c_q"), proj("c_kv"), proj("k_r")

    g_ref[...] = proj("g_a")
    u_ref[...] = proj("u_a")
    qd_ref[...] = (proj("q_d") * (DIFF_QK_DIM ** -0.5 * LOG2_E)).astype(BF16)
    kd_ref[...] = proj("k_d").astype(BF16)
    vdt_ref[...] = lax.dot_general(wvt_ref[...], xn, _NT, preferred_element_type=F32).astype(BF16)

    cos, slo, shi = cos_ref[...], slo_ref[...], shi_ref[...]
    cq = _rmsnorm(cq_raw, qn_ref[...]).astype(BF16)
    q = jnp.dot(cq, wuq_ref[...], preferred_element_type=F32)
    ckv = _rmsnorm(ckv_raw, kvn_ref[...]).astype(BF16)
    k_nope = jnp.dot(ckv, wuk_ref[...], preferred_element_type=F32)
    vct_ref[...] = lax.dot_general(wuvt_ref[...], ckv, _NT, preferred_element_type=F32).astype(BF16)
    k_rope = _rope(kr_raw, cos, slo, shi)
    q_scale = (MLA_NOPE_DIM + MLA_ROPE_DIM) ** -0.5 * LOG2_E
    for hd in range(MLA_HEADS):
        sl = slice(hd * LANES, (hd + 1) * LANES)
        qc_ref[:, sl] = (_rope(q[:, sl], cos, slo, shi) * q_scale).astype(BF16)
        kc_ref[:, sl] = (k_nope[:, sl] + k_rope).astype(BF16)


def _rope_tables(first_pos, t_len):
    half = MLA_ROPE_DIM // 2
    inv_freq = ROPE_THETA ** (-jnp.arange(half, dtype=F32) / half)
    pos = jnp.arange(first_pos, first_pos + t_len, dtype=jnp.int32)
    ang = pos.astype(F32)[:, None] * inv_freq[None, :]
    cos, sin = jnp.cos(ang), jnp.sin(ang)
    zeros = jnp.zeros((t_len, half), F32)
    ones = jnp.ones((t_len, MLA_NOPE_DIM), F32)
    tail = jnp.zeros((t_len, LANES - MLA_NOPE_DIM - MLA_ROPE_DIM), F32)
    cos_t = jnp.concatenate([ones, cos, cos, tail], axis=1)
    sin_lo = jnp.concatenate([0 * ones, -sin, zeros, tail], axis=1)
    sin_hi = jnp.concatenate([0 * ones, zeros, sin, tail], axis=1)
    return cos_t, sin_lo, sin_hi


def _pad_heads(w, heads, width, lo, hi):
    k = w.shape[0]
    w = w.reshape(k, heads, width)[:, :, lo:hi]
    w = jnp.pad(w, ((0, 0), (0, 0), (0, LANES - (hi - lo))))
    return w.reshape(k, heads * LANES)


def _mixer_proj(h3, gain, w_in, q_norm, w_uq, kv_norm, w_ukv, tables, *, blk):
    b, t, d = h3.shape
    assert t % blk == 0 and blk % LANES == 0
    widths = dict(g_a=256, u_a=256, q_d=512, k_d=512, v_d=512, c_q=w_uq.shape[0], c_kv=w_ukv.shape[0],
                  k_r=MLA_ROPE_DIM)
    offs, lo = {}, 0
    for name, wd in widths.items():
        offs[name] = (lo, lo + wd)
        lo += wd
    assert lo == w_in.shape[1]
    kr_lo, kr_hi = offs["k_r"]
    w_kr = jnp.pad(w_in[:, kr_lo:kr_hi], ((0, 0), (MLA_NOPE_DIM, LANES - MLA_NOPE_DIM - MLA_ROPE_DIM)))
    w_all = jnp.concatenate([w_in[:, :kr_lo], w_kr], axis=1).astype(BF16)
    offs["k_r"] = (kr_lo, kr_lo + LANES)
    v_lo, v_hi = offs["v_d"]
    w_vt = w_in[:, v_lo:v_hi].T.astype(BF16)
    qk = MLA_NOPE_DIM + MLA_ROPE_DIM
    w_uq_p = _pad_heads(w_uq, MLA_HEADS, qk, 0, qk).astype(BF16)
    w_uk_p = _pad_heads(w_ukv, MLA_HEADS, MLA_NOPE_DIM + MLA_V_DIM, 0, MLA_NOPE_DIM).astype(BF16)
    w_uv = w_ukv.reshape(-1, MLA_HEADS, MLA_NOPE_DIM + MLA_V_DIM)[:, :, MLA_NOPE_DIM:]
    w_uvt = w_uv.reshape(-1, MLA_HEADS * MLA_V_DIM).T.astype(BF16)

    def rows(width, dtype):
        return (pl.BlockSpec((None, blk, width), lambda i, j: (i, j, 0)),
                jax.ShapeDtypeStruct((b, t, width), dtype))

    def cols(width):
        return (pl.BlockSpec((None, None, width, blk), lambda i, j: (i, j, 0, 0)),
                jax.ShapeDtypeStruct((b, t // blk, width, blk), BF16))

    outs = [rows(256, F32), rows(256, F32), rows(512, BF16), rows(512, BF16), cols(512),
            rows(MLA_HEADS * LANES, BF16), rows(MLA_HEADS * LANES, BF16), cols(MLA_HEADS * MLA_V_DIM)]
    table_spec = pl.BlockSpec((blk, LANES), lambda i, j: (j, 0))
    return pl.pallas_call(
        functools.partial(_proj_kernel, offs=offs),
        grid=(b, t // blk),
        in_specs=[pl.BlockSpec((None, blk, d), lambda i, j: (i, j, 0)), _resident((1, d)),
                  _resident(w_all.shape), _resident(w_vt.shape), _resident((1, w_uq.shape[0])),
                  _resident(w_uq_p.shape), _resident((1, w_ukv.shape[0])), _resident(w_uk_p.shape),
                  _resident(w_uvt.shape), table_spec, table_spec, table_spec],
        out_specs=[o[0] for o in outs],
        out_shape=[o[1] for o in outs],
        compiler_params=_params("parallel", "parallel"),
        name="mixer_proj",
    )(h3, gain.reshape(1, d), w_all, w_vt, q_norm.reshape(1, -1), w_uq_p, kv_norm.reshape(1, -1), w_uk_p, w_uvt,
      *tables)


def _lru_kernel(g_ref, u_ref, cw_ref, cb_ref, wa_ref, ba_ref, wx_ref, bx_ref, lam_ref, on_ref, halo0_ref, carry0_ref,
                y_ref, halo_out_ref, carry_out_ref, halo_ref, carry_ref, a_scr, b_scr, h_scr, *, chunk):
    @pl.when(pl.program_id(1) == 0)
    def _():
        halo_ref[...] = halo0_ref[...]
        carry_ref[...] = carry0_ref[...]

    u = u_ref[...]
    ext = jnp.concatenate([halo_ref[...], u], axis=0)
    halo_ref[...] = u[chunk - SUBLANES:, :]
    conv = cb_ref[...] + cw_ref[CONV_WIDTH - 1:CONV_WIDTH, :] * u
    for j in range(CONV_WIDTH - 1):
        shifted = pltpu.roll(ext, CONV_WIDTH - 1 - j, 0)[SUBLANES:, :]
        conv = conv + cw_ref[j:j + 1, :] * shifted

    ub = conv.astype(BF16)
    r = jax.nn.sigmoid(jnp.dot(ub, wa_ref[...], preferred_element_type=F32) + ba_ref[...])
    i = jax.nn.sigmoid(jnp.dot(ub, wx_ref[...], preferred_element_type=F32) + bx_ref[...])
    neg_lam = -lam_ref[...]
    softplus = jnp.maximum(neg_lam, 0.0) + jnp.log1p(jnp.exp(-jnp.abs(neg_lam)))
    log_a = -LRU_C * r * softplus
    a = jnp.exp(log_a)
    gain2 = -jnp.tanh(log_a) * (a * a + 1.0)
    b = jnp.where(gain2 > 0.0, gain2 * lax.rsqrt(gain2), 0.0) * (i * conv)

    groups = (chunk // SUBLANES, SUBLANES, a.shape[-1])
    a, b = a.reshape(groups), b.reshape(groups)
    row = lax.broadcasted_iota(jnp.int32, groups, 1)
    for s in (1, 2, 4):
        keep = row >= s
        b = jnp.where(keep, a * pltpu.roll(b, s, 1) + b, b)
        a = jnp.where(keep, a * pltpu.roll(a, s, 1), a)
    a_scr[...] = a.reshape(chunk, -1)
    b_scr[...] = b.reshape(chunk, -1)

    def group(gi, carry):
        sl = pl.ds(pl.multiple_of(gi * SUBLANES, SUBLANES), SUBLANES)
        h = a_scr[sl, :] * carry + b_scr[sl, :]
        h_scr[sl, :] = h
        return jnp.broadcast_to(h[SUBLANES - 1:SUBLANES, :], h.shape)

    carry = lax.fori_loop(0, chunk // SUBLANES, group, carry_ref[...])
    carry_ref[...] = carry
    carry_out_ref[...] = carry
    halo_out_ref[...] = u[chunk - SUBLANES:, :]
    y = h_scr[...] * jax.nn.gelu(g_ref[...])
    y_ref[...] = _rmsnorm(y, on_ref[...]).astype(BF16)


def _block_diag(w):
    n, c, d = w.shape
    eye = jnp.eye(n, dtype=w.dtype)
    return (w[:, :, None, :] * eye[:, None, :, None]).reshape(n * c, n * d)


def _lru(g, u, conv_w, conv_b, wa, ba, wx, bx, lam, out_norm, *, chunk, halo0=None, carry0=None):
    b, t, c = u.shape
    assert t % chunk == 0 and chunk % (2 * SUBLANES) == 0
    if halo0 is None:
        halo0 = carry0 = jnp.zeros((SUBLANES, c), F32)
    row = lambda v: v.reshape(1, c)
    seq = pl.BlockSpec((None, chunk, c), lambda i, j: (i, j, 0))
    ctx = pl.BlockSpec((None, None, SUBLANES, c), lambda i, j: (i, j, 0, 0))
    ctx_shape = jax.ShapeDtypeStruct((b, t // chunk, SUBLANES, c), F32)
    return pl.pallas_call(
        functools.partial(_lru_kernel, chunk=chunk),
        grid=(b, t // chunk),
        in_specs=[seq, seq, _resident((CONV_WIDTH, c)), _resident((1, c)), _resident((c, c)), _resident((1, c)),
                  _resident((c, c)), _resident((1, c)), _resident((1, c)), _resident((1, c)),
                  _resident((SUBLANES, c)), _resident((SUBLANES, c))],
        out_specs=[seq, ctx, ctx],
        out_shape=[jax.ShapeDtypeStruct((b, t, c), BF16), ctx_shape, ctx_shape],
        scratch_shapes=[pltpu.VMEM((SUBLANES, c), F32), pltpu.VMEM((SUBLANES, c), F32),
                        pltpu.VMEM((chunk, c), F32), pltpu.VMEM((chunk, c), F32), pltpu.VMEM((chunk, c), F32)],
        compiler_params=_params("parallel", "arbitrary"),
        name="rg_lru",
    )(g, u, conv_w, row(conv_b), _block_diag(wa).astype(BF16), row(ba), _block_diag(wx).astype(BF16), row(bx),
      row(lam), row(out_norm), halo0, carry0)


def _attn_kernel(q_ref, k_ref, vt_ref, *rest, blk, ratio, mode, groups, lam_init, has_prefix):
    if has_prefix:
        kpre_ref, vtpre_ref, *rest = rest
    if mode == "diff":
        lam_ref, subln_ref, o_ref, m_scr, l_scr, acc_scr, s_scr, bm_scr = rest
    else:
        o_ref, m_scr, l_scr, acc_scr, s_scr, bm_scr = rest
    qi = pl.program_id(2)
    width = LANES if mode == "diff" else 2 * LANES
    v_rows = LANES if mode == "diff" else LANES // 2
    chains = [(g, c) for g in range(groups) for c in range(2)]

    def k_lanes(i):
        g, c = chains[i]
        lo = g * width + (0 if mode == "diff" else c * LANES)
        return slice(lo, lo + LANES)

    def v_sublanes(i):
        lo = (i // 2) * LANES if mode == "diff" else i * v_rows
        return slice(lo, lo + v_rows)

    qs = []
    for g in range(groups):
        if mode == "diff":
            q = q_ref[:, g * width:(g + 1) * width]
            low = lax.broadcasted_iota(jnp.int32, q.shape, 1) < (LANES // 2)
            zero = jnp.zeros_like(q)
            qs += [jnp.where(low, q, zero), jnp.where(low, zero, q)]
        else:
            qs += [q_ref[:, g * width:g * width + LANES], q_ref[:, g * width + LANES:(g + 1) * width]]

    def scores(kj, q_lo=0):
        rows = pl.ds(pl.multiple_of(kj * blk, blk), blk)
        return [lax.dot_general(k_ref[rows, k_lanes(i)], qs[i][q_lo:, :], _NT, preferred_element_type=F32)
                for i in range(len(chains))]

    def values(kj, i):
        return vt_ref[kj, v_sublanes(i), :]

    first = qi * ratio

    def produce(kj, slot, diag_r=None):
        q_lo = 0 if diag_r is None else diag_r * blk
        for i, s in enumerate(scores(kj, q_lo)):
            if diag_r is not None:
                key = lax.broadcasted_iota(jnp.int32, s.shape, 0)
                qry = lax.broadcasted_iota(jnp.int32, s.shape, 1)
                s = jnp.where(key <= qry, s, -jnp.inf)
            s_scr[slot, i, :, q_lo:] = s
            bm_scr[slot, i, :, q_lo:] = jnp.max(s, axis=0, keepdims=True)

    def consume(kj, slot, q_lo=0):
        for i in range(len(chains)):
            m_old = m_scr[i, :, q_lo:]
            m_new = jnp.maximum(m_old, bm_scr[slot, i, :, q_lo:])
            alpha = jnp.exp2(m_old - m_new)
            p = jnp.exp2(s_scr[slot, i, :, q_lo:] - m_new)
            l_scr[i, :, q_lo:] = alpha * l_scr[i, :, q_lo:] + jnp.sum(p, axis=0, keepdims=True)
            m_scr[i, :, q_lo:] = m_new
            acc_scr[i, :, q_lo:] = (alpha * acc_scr[i, :, q_lo:]
                                    + jnp.dot(values(kj, i), p.astype(BF16), preferred_element_type=F32))

    for r in range(ratio):
        produce(first + r, 2 + r, diag_r=r)
    if has_prefix:
        pre = [lax.dot_general(kpre_ref[:, k_lanes(i)], qs[i], _NT, preferred_element_type=F32)
               for i in range(len(chains))]
    produce(0, 0)

    for i in range(len(chains)):
        m = bm_scr[2, i]
        if has_prefix:
            m = jnp.maximum(m, jnp.max(pre[i], axis=0, keepdims=True))
        p = jnp.exp2(s_scr[2, i] - m)
        l = jnp.sum(p, axis=0, keepdims=True)
        acc = jnp.dot(values(first, i), p.astype(BF16), preferred_element_type=F32)
        if has_prefix:
            p_pre = jnp.exp2(pre[i] - m)
            l = l + jnp.sum(p_pre, axis=0, keepdims=True)
            acc = acc + jnp.dot(vtpre_ref[v_sublanes(i), :], p_pre.astype(BF16), preferred_element_type=F32)
        m_scr[i] = m
        l_scr[i] = l
        acc_scr[i] = acc
    for r in range(1, ratio):
        consume(first + r, 2 + r, q_lo=r * blk)

    def pair(i, carry):
        produce(2 * i + 1, 1)
        consume(2 * i, 0)
        produce(jnp.minimum(2 * i + 2, first), 0)
        consume(2 * i + 1, 1)
        return carry

    lax.fori_loop(0, first // 2, pair, 0)

    if ratio % 2 == 1:
        @pl.when(first % 2 == 1)
        def _():
            consume(first - 1, 0)

    if mode == "diff":
        p_ = lam_ref[...]
        lam = (jnp.exp(jnp.sum(p_[0:1] * p_[1:2], axis=-1, keepdims=True))
               - jnp.exp(jnp.sum(p_[2:3] * p_[3:4], axis=-1, keepdims=True)) + lam_init)
    for g in range(groups):
        o_a = acc_scr[2 * g] * (1.0 / l_scr[2 * g])
        o_b = acc_scr[2 * g + 1] * (1.0 / l_scr[2 * g + 1])
        cols = slice(g * LANES, (g + 1) * LANES)
        if mode == "diff":
            o = o_a - lam * o_b
            o = o * lax.rsqrt(jnp.mean(o * o, axis=0, keepdims=True) + NORM_EPS) * subln_ref[...]
            o_ref[:, cols] = (o * (1.0 - lam_init)).T.astype(o_ref.dtype)
        else:
            o_ref[:, cols] = jnp.concatenate([o_a, o_b], axis=0).T.astype(o_ref.dtype)


def _attention(q, k, vt, *, mode, prefix=None, lam_params=None, subln=None, lam_init=0.0, out_dtype=BF16,
               groups_per_step=2):
    b, t, _ = q.shape
    blk = vt.shape[3]
    width = LANES if mode == "diff" else 2 * LANES
    total = vt.shape[2] // LANES
    groups = min(total, groups_per_step)
    assert t % blk == 0 and q.shape[2] == total * width and total % groups == 0
    ratio = 2 if (t // blk) % 2 == 0 else 1
    blk_q = ratio * blk
    in_specs = [pl.BlockSpec((None, blk_q, groups * width), lambda i, g, j: (i, j, g)),
                pl.BlockSpec((None, t, groups * width), lambda i, g, j: (i, 0, g)),
                pl.BlockSpec((None, t // blk, groups * LANES, blk), lambda i, g, j: (i, 0, g, 0))]
    args = [q, k, vt]
    if prefix is not None:
        k_pre, vt_pre = prefix
        n_pre = k_pre.shape[0]
        in_specs += [pl.BlockSpec((n_pre, groups * width), lambda i, g, j: (0, g)),
                     pl.BlockSpec((groups * LANES, n_pre), lambda i, g, j: (g, 0))]
        args += [k_pre, vt_pre]
    if mode == "diff":
        in_specs += [_resident(lam_params.shape), _resident((LANES, blk_q))]
        args += [lam_params, jnp.broadcast_to(subln.reshape(LANES, 1), (LANES, blk_q))]
    chains = 2 * groups
    v_rows = LANES if mode == "diff" else LANES // 2
    return pl.pallas_call(
        functools.partial(_attn_kernel, blk=blk, ratio=ratio, mode=mode, groups=groups, lam_init=lam_init,
                          has_prefix=prefix is not None),
        grid=(b, total // groups, t // blk_q),
        in_specs=in_specs,
        out_specs=pl.BlockSpec((None, blk_q, groups * LANES), lambda i, g, j: (i, j, g)),
        out_shape=jax.ShapeDtypeStruct((b, t, total * LANES), out_dtype),
        scratch_shapes=[pltpu.VMEM((chains, 1, blk_q), F32), pltpu.VMEM((chains, 1, blk_q), F32),
                        pltpu.VMEM((chains, v_rows, blk_q), F32),
                        pltpu.VMEM((2 + ratio, chains, blk, blk_q), F32),
                        pltpu.VMEM((2 + ratio, chains, 1, blk_q), F32)],
        compiler_params=_params("parallel", "parallel", "arbitrary"),
        name="attn_" + mode,
    )(*args)


def kernel(x, meta_tokens, ffn1_norm, ffn1_in, ffn1_out, mix_norm, w_in, conv_w, conv_b, lru_wa, lru_ba, lru_wx, lru_bx, lru_lambda, lru_out_norm, lam_q1, lam_k1, lam_q2, lam_k2, diff_subln, q_norm, w_uq, kv_norm, w_ukv, mla_out_norm, w_out, ffn2_norm, ffn2_in, ffn2_out, final_norm):
    b, s, d = x.shape
    depth = ffn1_in.shape[0]
    t = -(-s // SEQ_BLOCK) * SEQ_BLOCK
    t_meta = LANES
    assert N_META % (2 * SUBLANES) == 0 and N_META <= t_meta
    h = jnp.pad(x, ((0, 0), (0, t - s), (0, 0))).reshape(b * t, d)
    h_meta = jnp.pad(meta_tokens.astype(x.dtype), ((0, t_meta - N_META), (0, 0)))
    tables = _rope_tables(N_META, t)
    tables_meta = _rope_tables(0, t_meta)
    lru_chunk = t // 4 if (t // 4) % (2 * SUBLANES) == 0 else SEQ_BLOCK

    def mixer(hs, l, *, seq, blk, chunk, rope, prefix):
        g_a, u_a, q_d, k_d, vt_d, q_c, k_c, vt_c = _mixer_proj(
            hs.reshape(-1, seq, d), mix_norm[l], w_in[l], q_norm[l], w_uq[l], kv_norm[l], w_ukv[l], rope, blk=blk)
        halo0, carry0, pre_d, pre_c = prefix if prefix is not None else (None,) * 4
        y_a, halo, carry = _lru(g_a, u_a, conv_w[l], conv_b[l], lru_wa[l], lru_ba[l], lru_wx[l], lru_bx[l],
                                lru_lambda[l], lru_out_norm[l], chunk=chunk, halo0=halo0, carry0=carry0)
        lam_init = 0.8 - 0.6 * math.exp(-0.3 * l)
        lam_params = jnp.stack([lam_q1[l], lam_k1[l], lam_q2[l], lam_k2[l]]).astype(F32)
        y_b = _attention(q_d, k_d, vt_d, mode="diff", prefix=pre_d, lam_params=lam_params, subln=diff_subln[l],
                         lam_init=lam_init)
        o_c = _attention(q_c, k_c, vt_c, mode="pair", prefix=pre_c, out_dtype=F32)
        export = (halo[0, 0], carry[0, 0], (k_d[0, :N_META], vt_d[0, 0, :, :N_META]),
                  (k_c[0, :N_META], vt_c[0, 0, :, :N_META]))
        n = hs.shape[0]
        return (y_a.reshape(n, -1), y_b.reshape(n, -1), o_c.reshape(n, -1)), export

    for l in range(depth):
        w1 = (ffn1_norm[l], ffn1_in[l].astype(BF16), ffn1_out[l].astype(BF16))
        w2 = (ffn2_norm[l], ffn2_in[l].astype(BF16), ffn2_out[l].astype(BF16))
        w_mix_out = w_out[l].astype(BF16)
        last = l == depth - 1
        h_meta = _ffn(h_meta, *w1, row_tile=t_meta)
        ys_meta, prefix = mixer(h_meta, l, seq=t_meta, blk=t_meta, chunk=N_META, rope=tables_meta, prefix=None)
        h = _ffn(h, *w1)
        ys, _ = mixer(h, l, seq=t, blk=SEQ_BLOCK, chunk=lru_chunk, rope=tables, prefix=prefix)
        h = _ffn(h, *w2, final_gain=final_norm if last else None, mixer=(*ys, mla_out_norm[l], w_mix_out))
        if not last:
            h_meta = _ffn(h_meta, *w2, mixer=(*ys_meta, mla_out_norm[l], w_mix_out), row_tile=t_meta)
    return h.reshape(b, t, d)[:, :s]
```

```python
import functools
import math

import jax
import jax.numpy as jnp
from jax import lax
from jax.experimental import pallas as pl
from jax.experimental.pallas import tpu as pltpu

F32 = jnp.float32
BF16 = jnp.bfloat16

N_META = 16
NORM_EPS = 1e-6
CONV_WIDTH = 4
LRU_C = 8.0
DIFF_QK_DIM = 64
MLA_HEADS = 4
MLA_NOPE_DIM = 64
MLA_ROPE_DIM = 32
MLA_V_DIM = 64
ROPE_THETA = 10000.0
LOG2_E = math.log2(math.e)

LANES = 128
SUBLANES = 8
MXU_DIM = 256
SEQ_BLOCK = MXU_DIM
VMEM_LIMIT_BYTES = 56 * 1024 * 1024

SUM_ROWS = 16

_NT = (((1,), (1,)), ((), ()))


def _params(*semantics):
    return pltpu.CompilerParams(dimension_semantics=semantics, vmem_limit_bytes=VMEM_LIMIT_BYTES)


def _resident(shape):
    zeros = (0,) * len(shape)
    return pl.BlockSpec(shape, lambda *_: zeros, pipeline_mode=pl.Buffered(1))


def _rmsnorm(x, gain):
    return x * lax.rsqrt(jnp.mean(x * x, axis=-1, keepdims=True) + NORM_EPS) * gain


def _mixer_out_rows(ya_ref, yb_ref, oc_ref, cn_ref, w_ref):
    a_hi = ya_ref.shape[1]
    b_hi = a_hi + yb_ref.shape[1]
    yc = _rmsnorm(oc_ref[...], cn_ref[...]).astype(BF16)
    acc = jnp.dot(ya_ref[...], w_ref[:a_hi, :], preferred_element_type=F32)
    acc += jnp.dot(yb_ref[...], w_ref[a_hi:b_hi, :], preferred_element_type=F32)
    return acc + jnp.dot(yc, w_ref[b_hi:, :], preferred_element_type=F32)


def _ffn_kernel(h_ref, gain_ref, w_in_ref, w_out_ref, *rest, d_ff, ff_chunk, final_norm, with_mixer):
    if with_mixer:
        mix_refs, rest = rest[:5], rest[5:]
    if final_norm:
        final_gain_ref, o_ref, acc_ref = rest
    else:
        o_ref, acc_ref = rest
    x = h_ref[...]
    if with_mixer:
        x = x + _mixer_out_rows(*mix_refs)
    xn = _rmsnorm(x, gain_ref[...]).astype(BF16)
    for j in range(d_ff // ff_chunk):
        lo = j * ff_chunk
        gate = jnp.dot(xn, w_in_ref[:, lo:lo + ff_chunk], preferred_element_type=F32)
        up = jnp.dot(xn, w_in_ref[:, d_ff + lo:d_ff + lo + ff_chunk], preferred_element_type=F32)
        act = (jax.nn.silu(gate) * up).astype(BF16)
        part = jnp.dot(act, w_out_ref[lo:lo + ff_chunk, :], preferred_element_type=F32)
        if j == 0:
            acc_ref[...] = part
        else:
            acc_ref[...] += part
    y = x + 0.5 * acc_ref[...]
    if final_norm:
        y = _rmsnorm(y, final_gain_ref[...])
    o_ref[...] = y


def _ffn(h, gain, w_in, w_out, final_gain=None, mixer=None, *, row_tile=512, ff_chunk=256):
    n, d = h.shape
    d_ff = w_out.shape[0]
    assert n % row_tile == 0 and d_ff % ff_chunk == 0
    final_norm = final_gain is not None
    rows = lambda width: pl.BlockSpec((row_tile, width), lambda i: (i, 0))
    in_specs = [rows(d), _resident((1, d)), _resident(w_in.shape), _resident(w_out.shape)]
    args = [h, gain.reshape(1, d), w_in, w_out]
    if mixer is not None:
        y_a, y_b, o_c, c_norm, w_mix = mixer
        assert y_a.shape[1] + y_b.shape[1] + o_c.shape[1] == w_mix.shape[0]
        in_specs += [rows(y_a.shape[1]), rows(y_b.shape[1]), rows(o_c.shape[1]), _resident((1, o_c.shape[1])),
                     _resident(w_mix.shape)]
        args += [y_a, y_b, o_c, c_norm.reshape(1, -1), w_mix]
    if final_norm:
        in_specs.append(_resident((1, d)))
        args.append(final_gain.reshape(1, d))
    return pl.pallas_call(
        functools.partial(_ffn_kernel, d_ff=d_ff, ff_chunk=ff_chunk, final_norm=final_norm,
                          with_mixer=mixer is not None),
        grid=(n // row_tile,),
        in_specs=in_specs,
        out_specs=pl.BlockSpec((row_tile, d), lambda i: (i, 0)),
        out_shape=jax.ShapeDtypeStruct((n, d), F32),
        scratch_shapes=[pltpu.VMEM((row_tile, d), F32)],
        compiler_params=_params("parallel"),
        name="ffn",
    )(*args)


def _rope(x, cos, sin_lo, sin_hi):
    half = MLA_ROPE_DIM // 2
    return x * cos + pltpu.roll(x, LANES - half, 1) * sin_lo + pltpu.roll(x, half, 1) * sin_hi


def _proj_kernel(h_ref, gain_ref, w_ref, wvt_ref, qn_ref, wuq_ref, kvn_ref, wuk_ref, wuvt_ref,
                 cos_ref, slo_ref, shi_ref,
                 g_ref, u_ref, qd_ref, kd_ref, vdt_ref, qc_ref, kc_ref, vct_ref, *, offs):
    xn = _rmsnorm(h_ref[...], gain_ref[...]).astype(BF16)

    def proj(name):
        lo, hi = offs[name]
        return jnp.dot(xn, w_ref[:, lo:hi], preferred_element_type=F32)

    cq_raw, ckv_raw, kr_raw = proj("c_q"), proj("c_kv"), proj("k_r")

    g_ref[...] = proj("g_a")
    u_ref[...] = proj("u_a")
    qd_ref[...] = (proj("q_d") * (DIFF_QK_DIM ** -0.5 * LOG2_E)).astype(BF16)
    kd_ref[...] = proj("k_d").astype(BF16)
    vdt_ref[...] = lax.dot_general(wvt_ref[...], xn, _NT, preferred_element_type=F32).astype(BF16)

    cos, slo, shi = cos_ref[...], slo_ref[...], shi_ref[...]
    cq = _rmsnorm(cq_raw, qn_ref[...]).astype(BF16)
    q = jnp.dot(cq, wuq_ref[...], preferred_element_type=F32)
    ckv = _rmsnorm(ckv_raw, kvn_ref[...]).astype(BF16)
    k_nope = jnp.dot(ckv, wuk_ref[...], preferred_element_type=F32)
    vct_ref[...] = lax.dot_general(wuvt_ref[...], ckv, _NT, preferred_element_type=F32).astype(BF16)
    k_rope = _rope(kr_raw, cos, slo, shi)
    q_scale = (MLA_NOPE_DIM + MLA_ROPE_DIM) ** -0.5 * LOG2_E
    for hd in range(MLA_HEADS):
        sl = slice(hd * LANES, (hd + 1) * LANES)
        qc_ref[:, sl] = (_rope(q[:, sl], cos, slo, shi) * q_scale).astype(BF16)
        kc_ref[:, sl] = (k_nope[:, sl] + k_rope).astype(BF16)


def _rope_tables(first_pos, t_len):
    half = MLA_ROPE_DIM // 2
    inv_freq = ROPE_THETA ** (-jnp.arange(half, dtype=F32) / half)
    pos = jnp.arange(first_pos, first_pos + t_len, dtype=jnp.int32)
    ang = pos.astype(F32)[:, None] * inv_freq[None, :]
    cos, sin = jnp.cos(ang), jnp.sin(ang)
    zeros = jnp.zeros((t_len, half), F32)
    ones = jnp.ones((t_len, MLA_NOPE_DIM), F32)
    tail = jnp.zeros((t_len, LANES - MLA_NOPE_DIM - MLA_ROPE_DIM), F32)
    cos_t = jnp.concatenate([ones, cos, cos, tail], axis=1)
    sin_lo = jnp.concatenate([0 * ones, -sin, zeros, tail], axis=1)
    sin_hi = jnp.concatenate([0 * ones, zeros, sin, tail], axis=1)
    return cos_t, sin_lo, sin_hi


def _pad_heads(w, heads, width, lo, hi):
    k = w.shape[0]
    w = w.reshape(k, heads, width)[:, :, lo:hi]
    w = jnp.pad(w, ((0, 0), (0, 0), (0, LANES - (hi - lo))))
    return w.reshape(k, heads * LANES)


def _mixer_proj(h3, gain, w_in, q_norm, w_uq, kv_norm, w_ukv, tables, *, blk):
    b, t, d = h3.shape
    assert t % blk == 0 and blk % LANES == 0
    widths = dict(g_a=256, u_a=256, q_d=512, k_d=512, v_d=512, c_q=w_uq.shape[0], c_kv=w_ukv.shape[0],
                  k_r=MLA_ROPE_DIM)
    offs, lo = {}, 0
    for name, wd in widths.items():
        offs[name] = (lo, lo + wd)
        lo += wd
    assert lo == w_in.shape[1]
    kr_lo, kr_hi = offs["k_r"]
    w_kr = jnp.pad(w_in[:, kr_lo:kr_hi], ((0, 0), (MLA_NOPE_DIM, LANES - MLA_NOPE_DIM - MLA_ROPE_DIM)))
    w_all = jnp.concatenate([w_in[:, :kr_lo], w_kr], axis=1).astype(BF16)
    offs["k_r"] = (kr_lo, kr_lo + LANES)
    v_lo, v_hi = offs["v_d"]
    w_vt = w_in[:, v_lo:v_hi].T.astype(BF16)
    qk = MLA_NOPE_DIM + MLA_ROPE_DIM
    w_uq_p = _pad_heads(w_uq, MLA_HEADS, qk, 0, qk).astype(BF16)
    w_uk_p = _pad_heads(w_ukv, MLA_HEADS, MLA_NOPE_DIM + MLA_V_DIM, 0, MLA_NOPE_DIM).astype(BF16)
    w_uv = w_ukv.reshape(-1, MLA_HEADS, MLA_NOPE_DIM + MLA_V_DIM)[:, :, MLA_NOPE_DIM:]
    w_uvt = w_uv.reshape(-1, MLA_HEADS * MLA_V_DIM).T.astype(BF16)

    def rows(width, dtype):
        return (pl.BlockSpec((None, blk, width), lambda i, j: (i, j, 0)),
                jax.ShapeDtypeStruct((b, t, width), dtype))

    def cols(width):
        return (pl.BlockSpec((None, None, width, blk), lambda i, j: (i, j, 0, 0)),
                jax.ShapeDtypeStruct((b, t // blk, width, blk), BF16))

    outs = [rows(256, F32), rows(256, F32), rows(512, BF16), rows(512, BF16), cols(512),
            rows(MLA_HEADS * LANES, BF16), rows(MLA_HEADS * LANES, BF16), cols(MLA_HEADS * MLA_V_DIM)]
    table_spec = pl.BlockSpec((blk, LANES), lambda i, j: (j, 0))
    return pl.pallas_call(
        functools.partial(_proj_kernel, offs=offs),
        grid=(b, t // blk),
        in_specs=[pl.BlockSpec((None, blk, d), lambda i, j: (i, j, 0)), _resident((1, d)),
                  _resident(w_all.shape), _resident(w_vt.shape), _resident((1, w_uq.shape[0])),
                  _resident(w_uq_p.shape), _resident((1, w_ukv.shape[0])), _resident(w_uk_p.shape),
                  _resident(w_uvt.shape), table_spec, table_spec, table_spec],
        out_specs=[o[0] for o in outs],
        out_shape=[o[1] for o in outs],
        compiler_params=_params("parallel", "parallel"),
        name="mixer_proj",
    )(h3, gain.reshape(1, d), w_all, w_vt, q_norm.reshape(1, -1), w_uq_p, kv_norm.reshape(1, -1), w_uk_p, w_uvt,
      *tables)


def _lru_kernel(g_ref, u_ref, cw_ref, cb_ref, wa_ref, ba_ref, wx_ref, bx_ref, lam_ref, on_ref, halo0_ref, carry0_ref,
                y_ref, halo_out_ref, carry_out_ref, halo_ref, carry_ref, a_scr, b_scr, h_scr, *, chunk):
    @pl.when(pl.program_id(1) == 0)
    def _():
        halo_ref[...] = halo0_ref[...]
        carry_ref[...] = carry0_ref[...]

    u = u_ref[...]
    ext = jnp.concatenate([halo_ref[...], u], axis=0)
    halo_ref[...] = u[chunk - SUBLANES:, :]
    conv = cb_ref[...] + cw_ref[CONV_WIDTH - 1:CONV_WIDTH, :] * u
    for j in range(CONV_WIDTH - 1):
        shifted = pltpu.roll(ext, CONV_WIDTH - 1 - j, 0)[SUBLANES:, :]
        conv = conv + cw_ref[j:j + 1, :] * shifted

    ub = conv.astype(BF16)
    r = jax.nn.sigmoid(jnp.dot(ub, wa_ref[...], preferred_element_type=F32) + ba_ref[...])
    i = jax.nn.sigmoid(jnp.dot(ub, wx_ref[...], preferred_element_type=F32) + bx_ref[...])
    neg_lam = -lam_ref[...]
    softplus = jnp.maximum(neg_lam, 0.0) + jnp.log1p(jnp.exp(-jnp.abs(neg_lam)))
    log_a = -LRU_C * r * softplus
    a = jnp.exp(log_a)
    gain2 = -jnp.tanh(log_a) * (a * a + 1.0)
    b = jnp.where(gain2 > 0.0, gain2 * lax.rsqrt(gain2), 0.0) * (i * conv)

    groups = (chunk // SUBLANES, SUBLANES, a.shape[-1])
    a, b = a.reshape(groups), b.reshape(groups)
    row = lax.broadcasted_iota(jnp.int32, groups, 1)
    for s in (1, 2, 4):
        keep = row >= s
        b = jnp.where(keep, a * pltpu.roll(b, s, 1) + b, b)
        a = jnp.where(keep, a * pltpu.roll(a, s, 1), a)
    a_scr[...] = a.reshape(chunk, -1)
    b_scr[...] = b.reshape(chunk, -1)

    def group(gi, carry):
        sl = pl.ds(pl.multiple_of(gi * SUBLANES, SUBLANES), SUBLANES)
        h = a_scr[sl, :] * carry + b_scr[sl, :]
        h_scr[sl, :] = h
        return jnp.broadcast_to(h[SUBLANES - 1:SUBLANES, :], h.shape)

    carry = lax.fori_loop(0, chunk // SUBLANES, group, carry_ref[...])
    carry_ref[...] = carry
    carry_out_ref[...] = carry
    halo_out_ref[...] = u[chunk - SUBLANES:, :]
    y = h_scr[...] * jax.nn.gelu(g_ref[...])
    y_ref[...] = _rmsnorm(y, on_ref[...]).astype(BF16)


def _block_diag(w):
    n, c, d = w.shape
    eye = jnp.eye(n, dtype=w.dtype)
    return (w[:, :, None, :] * eye[:, None, :, None]).reshape(n * c, n * d)


def _lru(g, u, conv_w, conv_b, wa, ba, wx, bx, lam, out_norm, *, chunk, halo0=None, carry0=None):
    b, t, c = u.shape
    assert t % chunk == 0 and chunk % (2 * SUBLANES) == 0
    if halo0 is None:
        halo0 = carry0 = jnp.zeros((SUBLANES, c), F32)
    row = lambda v: v.reshape(1, c)
    seq = pl.BlockSpec((None, chunk, c), lambda i, j: (i, j, 0))
    ctx = pl.BlockSpec((None, None, SUBLANES, c), lambda i, j: (i, j, 0, 0))
    ctx_shape = jax.ShapeDtypeStruct((b, t // chunk, SUBLANES, c), F32)
    return pl.pallas_call(
        functools.partial(_lru_kernel, chunk=chunk),
        grid=(b, t // chunk),
        in_specs=[seq, seq, _resident((CONV_WIDTH, c)), _resident((1, c)), _resident((c, c)), _resident((1, c)),
                  _resident((c, c)), _resident((1, c)), _resident((1, c)), _resident((1, c)),
                  _resident((SUBLANES, c)), _resident((SUBLANES, c))],
        out_specs=[seq, ctx, ctx],
        out_shape=[jax.ShapeDtypeStruct((b, t, c), BF16), ctx_shape, ctx_shape],
        scratch_shapes=[pltpu.VMEM((SUBLANES, c), F32), pltpu.VMEM((SUBLANES, c), F32),
                        pltpu.VMEM((chunk, c), F32), pltpu.VMEM((chunk, c), F32), pltpu.VMEM((chunk, c), F32)],
        compiler_params=_params("parallel", "arbitrary"),
        name="rg_lru",
    )(g, u, conv_w, row(conv_b), _block_diag(wa).astype(BF16), row(ba), _block_diag(wx).astype(BF16), row(bx),
      row(lam), row(out_norm), halo0, carry0)


def _attn_kernel(q_ref, k_ref, vt_ref, *rest, blk, ratio, mode, groups, lam_init, has_prefix):
    if has_prefix:
        kpre_ref, vtpre_ref, *rest = rest
    if mode == "diff":
        lam_ref, subln_ref, o_ref, m_scr, acc_scr, s_scr, bm_scr = rest
    else:
        o_ref, m_scr, acc_scr, s_scr, bm_scr = rest
    qi = pl.program_id(2)
    width = LANES if mode == "diff" else 2 * LANES
    v_rows = LANES if mode == "diff" else LANES // 2
    chains = [(g, c) for g in range(groups) for c in range(2)]

    def k_lanes(i):
        g, c = chains[i]
        lo = g * width + (0 if mode == "diff" else c * LANES)
        return slice(lo, lo + LANES)

    def v_sublanes(i):
        lo = (i // 2) * LANES if mode == "diff" else i * v_rows
        return slice(lo, lo + v_rows)

    qs = []
    for g in range(groups):
        if mode == "diff":
            q = q_ref[:, g * width:(g + 1) * width]
            low = lax.broadcasted_iota(jnp.int32, q.shape, 1) < (LANES // 2)
            zero = jnp.zeros_like(q)
            qs += [jnp.where(low, q, zero), jnp.where(low, zero, q)]
        else:
            qs += [q_ref[:, g * width:g * width + LANES], q_ref[:, g * width + LANES:(g + 1) * width]]

    def scores(kj, q_lo=0):
        rows = pl.ds(pl.multiple_of(kj * blk, blk), blk)
        return [lax.dot_general(k_ref[rows, k_lanes(i)], qs[i][q_lo:, :], _NT, preferred_element_type=F32)
                for i in range(len(chains))]

    def with_ones(v):
        return jnp.concatenate([v, jnp.ones((SUM_ROWS, v.shape[1]), v.dtype)], axis=0)

    def values(kj, i):
        return with_ones(vt_ref[kj, v_sublanes(i), :])

    first = qi * ratio

    def produce(kj, slot, diag_r=None):
        q_lo = 0 if diag_r is None else diag_r * blk
        for i, s in enumerate(scores(kj, q_lo)):
            if diag_r is not None:
                key = lax.broadcasted_iota(jnp.int32, s.shape, 0)
                qry = lax.broadcasted_iota(jnp.int32, s.shape, 1)
                s = jnp.where(key <= qry, s, -jnp.inf)
            s_scr[slot, i, :, q_lo:] = s
            bm_scr[slot, i, :, q_lo:] = jnp.max(s, axis=0, keepdims=True)

    def consume(kj, slot, q_lo=0):
        for i in range(len(chains)):
            m_old = m_scr[i, :, q_lo:]
            m_new = jnp.maximum(m_old, bm_scr[slot, i, :, q_lo:])
            alpha = jnp.exp2(m_old - m_new)
            p = jnp.exp2(s_scr[slot, i, :, q_lo:] - m_new)
            m_scr[i, :, q_lo:] = m_new
            acc_scr[i, :, q_lo:] = (alpha * acc_scr[i, :, q_lo:]
                                    + jnp.dot(values(kj, i), p.astype(BF16), preferred_element_type=F32))

    for r in range(ratio):
        produce(first + r, 2 + r, diag_r=r)
    if has_prefix:
        pre = [lax.dot_general(kpre_ref[:, k_lanes(i)], qs[i], _NT, preferred_element_type=F32)
               for i in range(len(chains))]
    produce(0, 0)

    for i in range(len(chains)):
        m = bm_scr[2, i]
        if has_prefix:
            m = jnp.maximum(m, jnp.max(pre[i], axis=0, keepdims=True))
        p = jnp.exp2(s_scr[2, i] - m)
        acc = jnp.dot(values(first, i), p.astype(BF16), preferred_element_type=F32)
        if has_prefix:
            p_pre = jnp.exp2(pre[i] - m)
            acc = acc + jnp.dot(with_ones(vtpre_ref[v_sublanes(i), :]), p_pre.astype(BF16),
                                preferred_element_type=F32)
        m_scr[i] = m
        acc_scr[i] = acc
    for r in range(1, ratio):
        consume(first + r, 2 + r, q_lo=r * blk)

    def pair(i, carry):
        produce(2 * i + 1, 1)
        consume(2 * i, 0)
        produce(jnp.minimum(2 * i + 2, first), 0)
        consume(2 * i + 1, 1)
        return carry

    lax.fori_loop(0, first // 2, pair, 0)

    if ratio % 2 == 1:
        @pl.when(first % 2 == 1)
        def _():
            consume(first - 1, 0)

    if mode == "diff":
        p_ = lam_ref[...]
        lam = (jnp.exp(jnp.sum(p_[0:1] * p_[1:2], axis=-1, keepdims=True))
               - jnp.exp(jnp.sum(p_[2:3] * p_[3:4], axis=-1, keepdims=True)) + lam_init)
    for g in range(groups):
        o_a = acc_scr[2 * g, :v_rows] * (1.0 / acc_scr[2 * g, v_rows:v_rows + 1])
        o_b = acc_scr[2 * g + 1, :v_rows] * (1.0 / acc_scr[2 * g + 1, v_rows:v_rows + 1])
        cols = slice(g * LANES, (g + 1) * LANES)
        if mode == "diff":
            o = o_a - lam * o_b
            o = o * lax.rsqrt(jnp.mean(o * o, axis=0, keepdims=True) + NORM_EPS) * subln_ref[...]
            o_ref[:, cols] = (o * (1.0 - lam_init)).T.astype(o_ref.dtype)
        else:
            o_ref[:, cols] = jnp.concatenate([o_a, o_b], axis=0).T.astype(o_ref.dtype)


def _attention(q, k, vt, *, mode, prefix=None, lam_params=None, subln=None, lam_init=0.0, out_dtype=BF16,
               groups_per_step=2):
    b, t, _ = q.shape
    blk = vt.shape[3]
    width = LANES if mode == "diff" else 2 * LANES
    total = vt.shape[2] // LANES
    groups = min(total, groups_per_step)
    assert t % blk == 0 and q.shape[2] == total * width and total % groups == 0
    ratio = 2 if (t // blk) % 2 == 0 else 1
    blk_q = ratio * blk
    in_specs = [pl.BlockSpec((None, blk_q, groups * width), lambda i, g, j: (i, j, g)),
                pl.BlockSpec((None, t, groups * width), lambda i, g, j: (i, 0, g)),
                pl.BlockSpec((None, t // blk, groups * LANES, blk), lambda i, g, j: (i, 0, g, 0))]
    args = [q, k, vt]
    if prefix is not None:
        k_pre, vt_pre = prefix
        n_pre = k_pre.shape[0]
        in_specs += [pl.BlockSpec((n_pre, groups * width), lambda i, g, j: (0, g)),
                     pl.BlockSpec((groups * LANES, n_pre), lambda i, g, j: (g, 0))]
        args += [k_pre, vt_pre]
    if mode == "diff":
        in_specs += [_resident(lam_params.shape), _resident((LANES, blk_q))]
        args += [lam_params, jnp.broadcast_to(subln.reshape(LANES, 1), (LANES, blk_q))]
    chains = 2 * groups
    v_rows = LANES if mode == "diff" else LANES // 2
    return pl.pallas_call(
        functools.partial(_attn_kernel, blk=blk, ratio=ratio, mode=mode, groups=groups, lam_init=lam_init,
                          has_prefix=prefix is not None),
        grid=(b, total // groups, t // blk_q),
        in_specs=in_specs,
        out_specs=pl.BlockSpec((None, blk_q, groups * LANES), lambda i, g, j: (i, j, g)),
        out_shape=jax.ShapeDtypeStruct((b, t, total * LANES), out_dtype),
        scratch_shapes=[pltpu.VMEM((chains, 1, blk_q), F32),
                        pltpu.VMEM((chains, v_rows + SUM_ROWS, blk_q), F32),
                        pltpu.VMEM((2 + ratio, chains, blk, blk_q), F32),
                        pltpu.VMEM((2 + ratio, chains, 1, blk_q), F32)],
        compiler_params=_params("parallel", "parallel", "arbitrary"),
        name="attn_" + mode,
    )(*args)


def kernel(x, meta_tokens, ffn1_norm, ffn1_in, ffn1_out, mix_norm, w_in, conv_w, conv_b, lru_wa, lru_ba, lru_wx, lru_bx, lru_lambda, lru_out_norm, lam_q1, lam_k1, lam_q2, lam_k2, diff_subln, q_norm, w_uq, kv_norm, w_ukv, mla_out_norm, w_out, ffn2_norm, ffn2_in, ffn2_out, final_norm):
    b, s, d = x.shape
    depth = ffn1_in.shape[0]
    t = -(-s // SEQ_BLOCK) * SEQ_BLOCK
    t_meta = LANES
    assert N_META % (2 * SUBLANES) == 0 and N_META <= t_meta
    h = jnp.pad(x, ((0, 0), (0, t - s), (0, 0))).reshape(b * t, d)
    h_meta = jnp.pad(meta_tokens.astype(x.dtype), ((0, t_meta - N_META), (0, 0)))
    tables = _rope_tables(N_META, t)
    tables_meta = _rope_tables(0, t_meta)
    lru_chunk = t // 4 if (t // 4) % (2 * SUBLANES) == 0 else SEQ_BLOCK

    def mixer(hs, l, *, seq, blk, chunk, rope, prefix):
        g_a, u_a, q_d, k_d, vt_d, q_c, k_c, vt_c = _mixer_proj(
            hs.reshape(-1, seq, d), mix_norm[l], w_in[l], q_norm[l], w_uq[l], kv_norm[l], w_ukv[l], rope, blk=blk)
        halo0, carry0, pre_d, pre_c = prefix if prefix is not None else (None,) * 4
        y_a, halo, carry = _lru(g_a, u_a, conv_w[l], conv_b[l], lru_wa[l], lru_ba[l], lru_wx[l], lru_bx[l],
                                lru_lambda[l], lru_out_norm[l], chunk=chunk, halo0=halo0, carry0=carry0)
        lam_init = 0.8 - 0.6 * math.exp(-0.3 * l)
        lam_params = jnp.stack([lam_q1[l], lam_k1[l], lam_q2[l], lam_k2[l]]).astype(F32)
        y_b = _attention(q_d, k_d, vt_d, mode="diff", prefix=pre_d, lam_params=lam_params, subln=diff_subln[l],
                         lam_init=lam_init)
        o_c = _attention(q_c, k_c, vt_c, mode="pair", prefix=pre_c, out_dtype=F32)
        export = (halo[0, 0], carry[0, 0], (k_d[0, :N_META], vt_d[0, 0, :, :N_META]),
                  (k_c[0, :N_META], vt_c[0, 0, :, :N_META]))
        n = hs.shape[0]
        return (y_a.reshape(n, -1), y_b.reshape(n, -1), o_c.reshape(n, -1)), export

    for l in range(depth):
        w1 = (ffn1_norm[l], ffn1_in[l].astype(BF16), ffn1_out[l].astype(BF16))
        w2 = (ffn2_norm[l], ffn2_in[l].astype(BF16), ffn2_out[l].astype(BF16))
        w_mix_out = w_out[l].astype(BF16)
        last = l == depth - 1
        h_meta = _ffn(h_meta, *w1, row_tile=t_meta)
        ys_meta, prefix = mixer(h_meta, l, seq=t_meta, blk=t_meta, chunk=N_META, rope=tables_meta, prefix=None)
        h = _ffn(h, *w1)
        ys, _ = mixer(h, l, seq=t, blk=SEQ_BLOCK, chunk=lru_chunk, rope=tables, prefix=prefix)
        h = _ffn(h, *w2, final_gain=final_norm if last else None, mixer=(*ys, mla_out_norm[l], w_mix_out))
        if not last:
            h_meta = _ffn(h_meta, *w2, mixer=(*ys_meta, mla_out_norm[l], w_mix_out), row_tile=t_meta)
    return h.reshape(b, t, d)[:, :s]
```
